```python
import math
import jax, jax.numpy as jnp
from jax import lax
import numpy as np

D_MODEL = 2048
BATCH = 2
SEQ = 8192
DEPTH = 4

LRU_WIDTH = D_MODEL // 2
LRU_BLOCKS = 16
LRU_BLOCK = LRU_WIDTH // LRU_BLOCKS
LRU_CONV = 4
LRU_C = 8.0
HG_WIDTH = D_MODEL // 2
HG_EXPAND = 128
HG_HEADS = HG_WIDTH // HG_EXPAND
HG_DK = HG_EXPAND
HG_DV = HG_WIDTH // HG_HEADS
HG_CHUNK = 64
EVEN_IN = 2 * LRU_WIDTH + 4 * HG_WIDTH
MIX_WIDTH = LRU_WIDTH + HG_WIDTH
SSD_INNER = 2 * D_MODEL
SSD_HEADDIM = 64
SSD_HEADS = SSD_INNER // SSD_HEADDIM
SSD_GROUPS = 8
SSD_HPG = SSD_HEADS // SSD_GROUPS
SSD_STATE = 128
SSD_CONV = 4
SSD_CHUNK = 64
SSD_CONV_DIM = SSD_INNER + 2 * SSD_GROUPS * SSD_STATE
ODD_IN = SSD_INNER + SSD_CONV_DIM + SSD_HEADS
D_FF = 5632
FFN_CONV = 3
EPS = 1e-6
N_EVEN = (DEPTH + 1) // 2
N_ODD = DEPTH // 2

kernel_name = 'hybrid_rglru_hgrn2_mamba2_convffn'


def rmsnorm(x, w):
    xf = x.astype(jnp.float32)
    y = xf * lax.rsqrt(jnp.mean(xf * xf, axis=-1, keepdims=True) + EPS)
    return (y * w.astype(jnp.float32)).astype(x.dtype)


def causal_dwconv(x, w, b):
    k, c = w.shape
    y = lax.conv_general_dilated(x, w.astype(x.dtype)[:, None, :], window_strides=(1,),
                                 padding=[(k - 1, 0)], dimension_numbers=('NWC', 'WIO', 'NWC'),
                                 feature_group_count=c)
    return y + b.astype(x.dtype)


def to_chunks(t, chunk):
    return t.reshape(t.shape[0], t.shape[1] // chunk, chunk, *t.shape[2:])


def rg_lru(x, w_r, b_r, w_i, b_i, lam):
    bsz, s, w = x.shape
    xf = x.astype(jnp.float32)
    xb = xf.reshape(bsz, s, LRU_BLOCKS, LRU_BLOCK)
    r = jax.nn.sigmoid(jnp.einsum('bshi,hij->bshj', xb, w_r.astype(jnp.float32)).reshape(bsz, s, w) + b_r)
    gi = jax.nn.sigmoid(jnp.einsum('bshi,hij->bshj', xb, w_i.astype(jnp.float32)).reshape(bsz, s, w) + b_i)
    log_a = -LRU_C * r * jax.nn.softplus(-lam.astype(jnp.float32))
    a = jnp.exp(log_a)
    u = jnp.sqrt(-jnp.expm1(2.0 * log_a)) * (gi * xf)

    def combine(lhs, rhs):
        a1, b1 = lhs
        a2, b2 = rhs
        return a1 * a2, a2 * b1 + b2

    _, h = lax.associative_scan(combine, (a, u), axis=1)
    return h


def hgrn2(q, f_raw, v, lb):
    bsz, s, _ = q.shape
    lb = lb.astype(jnp.float32)
    f = lb + (1.0 - lb) * jax.nn.sigmoid(f_raw.astype(jnp.float32))
    k = 1.0 - f
    g = jnp.log(f)
    qs = jax.nn.silu(q.astype(jnp.float32)) * (HG_DK ** -0.5)
    hd = (HG_HEADS, HG_DK)
    qc = to_chunks(qs.reshape(bsz, s, *hd), HG_CHUNK)
    kc = to_chunks(k.reshape(bsz, s, *hd), HG_CHUNK)
    gc = to_chunks(g.reshape(bsz, s, *hd), HG_CHUNK)
    vc = to_chunks(v.astype(jnp.float32).reshape(bsz, s, HG_HEADS, HG_DV), HG_CHUNK)
    cum = jnp.cumsum(gc, axis=2)
    tot = cum[:, :, -1]
    mid = cum[:, :, HG_CHUNK // 2 - 1][:, :, None]
    causal = jnp.tril(jnp.ones((HG_CHUNK, HG_CHUNK), dtype=bool))
    scores = jnp.einsum('bnthd,bnshd->bnhts', qc * jnp.exp(cum - mid), kc * jnp.exp(mid - cum))
    scores = jnp.where(causal, scores, 0.0)
    o_intra = jnp.einsum('bnhts,bnshv->bnthv', scores, vc)

    def step(state, inp):
        q_c, k_c, v_c, cum_c, tot_c = inp
        o = jnp.einsum('bthd,bhdv->bthv', q_c * jnp.exp(cum_c), state)
        k_dec = k_c * jnp.exp(tot_c[:, None] - cum_c)
        new = jnp.exp(tot_c)[..., None] * state + jnp.einsum('bshd,bshv->bhdv', k_dec, v_c)
        return new, o

    state0 = jnp.zeros((bsz, HG_HEADS, HG_DK, HG_DV), jnp.float32)
    xs = tuple(jnp.moveaxis(t, 1, 0) for t in (qc, kc, vc, cum, tot))
    _, o_inter = lax.scan(step, state0, xs)
    o = o_intra + jnp.moveaxis(o_inter, 0, 1)
    return o.reshape(bsz, s, HG_HEADS, HG_DV)


def even_mixer(h, w_in, lru_conv_w, lru_conv_b, lru_w_r, lru_b_r, lru_w_i, lru_b_i, lru_lambda,
               lb, hg_norm_w, w_out):
    bsz, s, _ = h.shape
    proj = h @ w_in
    cuts = [LRU_WIDTH, 2 * LRU_WIDTH, 2 * LRU_WIDTH + HG_WIDTH,
            2 * LRU_WIDTH + 2 * HG_WIDTH, 2 * LRU_WIDTH + 3 * HG_WIDTH]
    xa, ga, qb, fb, ib, gb = jnp.split(proj, cuts, axis=-1)
    ha = rg_lru(causal_dwconv(xa, lru_conv_w, lru_conv_b), lru_w_r, lru_b_r, lru_w_i, lru_b_i, lru_lambda)
    ya = ha * jax.nn.gelu(ga.astype(jnp.float32))
    ob = hgrn2(qb, fb, ib, lb)
    ob = ob * lax.rsqrt(jnp.mean(ob * ob, axis=-1, keepdims=True) + EPS) * hg_norm_w.astype(jnp.float32)
    yb = ob.reshape(bsz, s, HG_WIDTH) * jax.nn.silu(gb.astype(jnp.float32))
    y = jnp.concatenate([ya, yb], axis=-1).astype(h.dtype)
    return y @ w_out


def ssd_scan(x, dt, a_neg, bm, cm):
    bsz, s, _ = x.shape
    xc = to_chunks(x.reshape(bsz, s, SSD_GROUPS, SSD_HPG, SSD_HEADDIM), SSD_CHUNK)
    dtc = to_chunks(dt.reshape(bsz, s, SSD_GROUPS, SSD_HPG), SSD_CHUNK)
    bc = to_chunks(bm, SSD_CHUNK)
    cc = to_chunks(cm, SSD_CHUNK)
    xdt = xc * dtc[..., None]
    cum = jnp.cumsum(dtc * a_neg.reshape(SSD_GROUPS, SSD_HPG), axis=2)
    tot = cum[:, :, -1]
    causal = jnp.tril(jnp.ones((SSD_CHUNK, SSD_CHUNK), dtype=bool))[:, :, None, None]
    seg = cum[:, :, :, None] - cum[:, :, None, :]
    decay = jnp.exp(jnp.where(causal, seg, -jnp.inf))
    cb = jnp.einsum('bntgk,bnsgk->bntsg', cc, bc)
    y_diag = jnp.einsum('bntsgh,bnsghp->bntghp', cb[..., None] * decay, xdt)

    def step(state, inp):
        c_c, b_c, xdt_c, cum_c, tot_c = inp
        y_off = jnp.einsum('btgk,bghpk->btghp', c_c, state) * jnp.exp(cum_c)[..., None]
        dec = jnp.exp(tot_c[:, None] - cum_c)
        new = jnp.exp(tot_c)[..., None, None] * state + jnp.einsum('bsgk,bsghp->bghpk', b_c, xdt_c * dec[..., None])
        return new, y_off

    state0 = jnp.zeros((bsz, SSD_GROUPS, SSD_HPG, SSD_HEADDIM, SSD_STATE), jnp.float32)
    xs = tuple(jnp.moveaxis(t, 1, 0) for t in (cc, bc, xdt, cum, tot))
    _, y_off = lax.scan(step, state0, xs)
    y = y_diag + jnp.moveaxis(y_off, 0, 1)
    return y, xc


def ssd_mixer(h, w_in, conv_w, conv_b, dt_bias, a_log, d_skip, norm_w, w_out):
    bsz, s, _ = h.shape
    proj = h @ w_in
    z, xbc, dt_raw = jnp.split(proj, [SSD_INNER, SSD_INNER + SSD_CONV_DIM], axis=-1)
    xbc = jax.nn.silu(causal_dwconv(xbc, conv_w, conv_b)).astype(jnp.float32)
    xs, bm, cm = jnp.split(xbc, [SSD_INNER, SSD_INNER + SSD_GROUPS * SSD_STATE], axis=-1)
    bm = bm.reshape(bsz, s, SSD_GROUPS, SSD_STATE)
    cm = cm.reshape(bsz, s, SSD_GROUPS, SSD_STATE)
    dt = jax.nn.softplus(dt_raw.astype(jnp.float32) + dt_bias.astype(jnp.float32))
    a_neg = -jnp.exp(a_log.astype(jnp.float32))
    y, xc = ssd_scan(xs, dt, a_neg, bm, cm)
    y = y + xc * d_skip.astype(jnp.float32).reshape(SSD_GROUPS, SSD_HPG)[..., None]
    y = y.reshape(bsz, s, SSD_INNER) * jax.nn.silu(z.astype(jnp.float32))
    y = y.reshape(bsz, s, SSD_GROUPS, SSD_INNER // SSD_GROUPS)
    y = y * lax.rsqrt(jnp.mean(y * y, axis=-1, keepdims=True) + EPS)
    y = y.reshape(bsz, s, SSD_INNER) * norm_w.astype(jnp.float32)
    return y.astype(h.dtype) @ w_out


def conv_ffn(h, w_up, conv_w, conv_b, w_down):
    u = causal_dwconv(h @ w_up, conv_w, conv_b)
    gate, val = jnp.split(u, 2, axis=-1)
    return (jax.nn.silu(gate) * val) @ w_down


def setup_inputs(seed: int = 0) -> dict:
    key = jax.random.key(seed)
    ks = iter(jax.random.split(key, 48))

    def nrm(shape, scale):
        return scale * jax.random.normal(next(ks), shape, jnp.float32)

    def gain(shape):
        return 1.0 + 0.02 * jax.random.normal(next(ks), shape, jnp.float32)

    lam_u = jax.random.uniform(next(ks), (N_EVEN, LRU_WIDTH), jnp.float32, minval=0.9, maxval=0.999)
    dt0 = jnp.exp(jax.random.uniform(next(ks), (N_ODD, SSD_HEADS), jnp.float32,
                                     minval=math.log(1e-3), maxval=math.log(1e-1)))
    a0 = jax.random.uniform(next(ks), (N_ODD, SSD_HEADS), jnp.float32, minval=1.0, maxval=16.0)
    return {
        'x': nrm((BATCH, SEQ, D_MODEL), 1.0),
        'norm_mix_w': gain((DEPTH, D_MODEL)),
        'norm_ffn_w': gain((DEPTH, D_MODEL)),
        'norm_final_w': gain((D_MODEL,)),
        'ev_w_in': nrm((N_EVEN, D_MODEL, EVEN_IN), D_MODEL ** -0.5),
        'lru_conv_w': nrm((N_EVEN, LRU_CONV, LRU_WIDTH), LRU_CONV ** -0.5),
        'lru_conv_b': nrm((N_EVEN, LRU_WIDTH), 0.02),
        'lru_w_r': nrm((N_EVEN, LRU_BLOCKS, LRU_BLOCK, LRU_BLOCK), LRU_BLOCK ** -0.5),
        'lru_b_r': nrm((N_EVEN, LRU_WIDTH), 0.02),
        'lru_w_i': nrm((N_EVEN, LRU_BLOCKS, LRU_BLOCK, LRU_BLOCK), LRU_BLOCK ** -0.5),
        'lru_b_i': nrm((N_EVEN, LRU_WIDTH), 0.02),
        'lru_lambda': jnp.log(lam_u) - jnp.log1p(-lam_u),
        'hg_lower_bounds': nrm((N_EVEN, HG_WIDTH), 0.02),
        'hg_norm_w': gain((N_EVEN, HG_DV)),
        'ev_w_out': nrm((N_EVEN, MIX_WIDTH, D_MODEL), MIX_WIDTH ** -0.5),
        'ssd_w_in': nrm((N_ODD, D_MODEL, ODD_IN), D_MODEL ** -0.5),
        'ssd_conv_w': nrm((N_ODD, SSD_CONV, SSD_CONV_DIM), SSD_CONV ** -0.5),
        'ssd_conv_b': nrm((N_ODD, SSD_CONV_DIM), 0.02),
        'ssd_dt_bias': dt0 + jnp.log(-jnp.expm1(-dt0)),
        'ssd_a_log': jnp.log(a0),
        'ssd_d': gain((N_ODD, SSD_HEADS)),
        'ssd_norm_w': gain((N_ODD, SSD_INNER)),
        'ssd_w_out': nrm((N_ODD, SSD_INNER, D_MODEL), SSD_INNER ** -0.5),
        'ffn_w_up': nrm((DEPTH, D_MODEL, 2 * D_FF), D_MODEL ** -0.5),
        'ffn_conv_w': nrm((DEPTH, FFN_CONV, 2 * D_FF), FFN_CONV ** -0.5),
        'ffn_conv_b': nrm((DEPTH, 2 * D_FF), 0.02),
        'ffn_w_down': nrm((DEPTH, D_FF, D_MODEL), D_FF ** -0.5),
    }


def reference(x, norm_mix_w, norm_ffn_w, norm_final_w, ev_w_in, lru_conv_w, lru_conv_b, lru_w_r,
              lru_b_r, lru_w_i, lru_b_i, lru_lambda, hg_lower_bounds, hg_norm_w, ev_w_out,
              ssd_w_in, ssd_conv_w, ssd_conv_b, ssd_dt_bias, ssd_a_log, ssd_d, ssd_norm_w, ssd_w_out,
              ffn_w_up, ffn_conv_w, ffn_conv_b, ffn_w_down):
    lbp = jax.nn.softmax(hg_lower_bounds.astype(jnp.float32), axis=0)
    lbs = jnp.cumsum(lbp, axis=0) - lbp[0]
    for l in range(DEPTH):
        h = rmsnorm(x, norm_mix_w[l])
        if l % 2 == 0:
            e = l // 2
            mix = even_mixer(h, ev_w_in[e], lru_conv_w[e], lru_conv_b[e], lru_w_r[e], lru_b_r[e],
                             lru_w_i[e], lru_b_i[e], lru_lambda[e], lbs[e], hg_norm_w[e], ev_w_out[e])
        else:
            o = l // 2
            mix = ssd_mixer(h, ssd_w_in[o], ssd_conv_w[o], ssd_conv_b[o], ssd_dt_bias[o], ssd_a_log[o],
                            ssd_d[o], ssd_norm_w[o], ssd_w_out[o])
        x = x + mix
        x = x + conv_ffn(rmsnorm(x, norm_ffn_w[l]), ffn_w_up[l], ffn_conv_w[l], ffn_conv_b[l], ffn_w_down[l])
    return rmsnorm(x, norm_final_w)
```

```python
import functools

import jax
import jax.numpy as jnp
from jax import lax
from jax.experimental import pallas as pl
from jax.experimental.pallas import tpu as pltpu

F32 = jnp.float32
BF16 = jnp.bfloat16
EPS = 1e-6

V7X_LANES = 128
V7X_SUBLANES = 8
V7X_MXU_DIM = 256
V7X_VMEM_BYTES = 64 * 1024 * 1024
VMEM_LIMIT = V7X_VMEM_BYTES - 8 * 1024 * 1024

LRU_BLOCKS = 16
LRU_C = 8.0
HG_HEADS = 8
HG_DK = 128
HG_CHUNK = 64
SSD_HEADDIM = 64
SSD_GROUPS = 8
SSD_HPG = 8
SSD_STATE = 128
SSD_CHUNK = 128
NEG_BIG = -1e30


def _cparams(sem):
    return pltpu.CompilerParams(dimension_semantics=sem, vmem_limit_bytes=VMEM_LIMIT)


def _sigmoid(x):
    return 1.0 / (1.0 + jnp.exp(-x))


def _silu(x):
    return x * _sigmoid(x)


def _gelu_tanh(x):
    return 0.5 * x * (1.0 + jnp.tanh(0.7978845608028654 * (x + 0.044715 * (x * x * x))))


def _split3(x):
    hi = x.astype(BF16)
    r1 = x - hi.astype(F32)
    mid = r1.astype(BF16)
    lo = (r1 - mid.astype(F32)).astype(BF16)
    return jnp.concatenate([hi, mid, lo], axis=1)


def _cumsum_rows(x):
    n, w = x.shape
    r = lax.broadcasted_iota(jnp.int32, (n, n), 0)
    c = lax.broadcasted_iota(jnp.int32, (n, n), 1)
    tril = jnp.where(r >= c, 1.0, 0.0).astype(BF16)
    p = jnp.dot(tril, _split3(x), preferred_element_type=F32)
    return p[:, :w] + p[:, w:2 * w] + p[:, 2 * w:]


def _norm_mm_kernel(x_ref, nw_ref, w_ref, o_ref, h_ref, *, rows):
    @pl.when(pl.program_id(1) == 0)
    def _():
        def body(r, c):
            sl = pl.ds(pl.multiple_of(r * rows, rows), rows)
            x = x_ref[sl, :]
            ms = jnp.mean(x * x, axis=-1, keepdims=True)
            h_ref[sl, :] = (x * lax.rsqrt(ms + EPS) * nw_ref[...]).astype(BF16)
            return c
        lax.fori_loop(0, x_ref.shape[0] // rows, body, 0)

    o_ref[...] = jnp.dot(h_ref[...], w_ref[...], preferred_element_type=F32).astype(o_ref.dtype)


def norm_matmul(x, nw, w, *, tm, tn, out_dtype=F32):
    T, D = x.shape
    N = w.shape[1]
    tm = min(tm, T)
    return pl.pallas_call(
        functools.partial(_norm_mm_kernel, rows=128),
        grid=(T // tm, N // tn),
        in_specs=[pl.BlockSpec((tm, D), lambda i, j: (i, 0)),
                  pl.BlockSpec((1, D), lambda i, j: (0, 0)),
                  pl.BlockSpec((D, tn), lambda i, j: (0, j))],
        out_specs=pl.BlockSpec((tm, tn), lambda i, j: (i, j)),
        out_shape=jax.ShapeDtypeStruct((T, N), out_dtype),
        scratch_shapes=[pltpu.VMEM((tm, D), BF16)],
        compiler_params=_cparams(("parallel", "arbitrary")),
        name="norm_matmul",
    )(x, nw.reshape(1, D), w)


def _mm_res_kernel(*refs, nl):
    lhs = refs[:nl]
    w_ref, r_ref, o_ref = refs[nl], refs[nl + 1], refs[nl + 2]
    acc = r_ref[...]
    k0 = 0
    for l in lhs:
        k = l.shape[1]
        acc = acc + jnp.dot(l[...], w_ref[k0:k0 + k, :], preferred_element_type=F32)
        k0 += k
    o_ref[...] = acc


def matmul_residual(lhs_list, w, res, *, tm, tn):
    T, N = res.shape
    K = w.shape[0]
    tm = min(tm, T)
    nl = len(lhs_list)
    in_specs = [pl.BlockSpec((tm, l.shape[1]), lambda j, i: (i, 0)) for l in lhs_list]
    in_specs += [pl.BlockSpec((K, tn), lambda j, i: (0, j)),
                 pl.BlockSpec((tm, tn), lambda j, i: (i, j))]
    return pl.pallas_call(
        functools.partial(_mm_res_kernel, nl=nl),
        grid=(N // tn, T // tm),
        in_specs=in_specs,
        out_specs=pl.BlockSpec((tm, tn), lambda j, i: (i, j)),
        out_shape=jax.ShapeDtypeStruct((T, N), F32),
        compiler_params=_cparams(("parallel", "parallel")),
        name="matmul_residual",
    )(*lhs_list, w, res)


def _ffn_act_kernel(g_ref, v_ref, gh_ref, vh_ref, wg_ref, wv_ref, bg_ref, bv_ref, o_ref,
                    gpad, vpad, *, tt, seq, kw):
    first = (pl.program_id(0) * tt) % seq == 0

    def conv(x_ref, h_ref, pad, w_ref, b_ref):
        pad[0:8, :] = jnp.where(first, 0.0, h_ref[...])
        pad[8:8 + tt, :] = x_ref[...]
        y = b_ref[...]
        for k in range(kw):
            y = y + w_ref[k:k + 1, :] * pad[pl.ds(8 - (kw - 1) + k, tt), :]
        return y

    g = conv(g_ref, gh_ref, gpad, wg_ref, bg_ref)
    v = conv(v_ref, vh_ref, vpad, wv_ref, bv_ref)
    o_ref[...] = (_silu(g) * v).astype(o_ref.dtype)


def ffn_act(u, cw, cb, *, seq, tt=512, tc=512):
    T, F2 = u.shape
    F = F2 // 2
    kw = cw.shape[0]
    tt = min(tt, seq)
    nj = F // tc
    hb = tt // 8

    def halo(i, j):
        return (jnp.maximum(i * hb - 1, 0), j)

    def halo_v(i, j):
        return (jnp.maximum(i * hb - 1, 0), j + nj)

    return pl.pallas_call(
        functools.partial(_ffn_act_kernel, tt=tt, seq=seq, kw=kw),
        grid=(T // tt, nj),
        in_specs=[pl.BlockSpec((tt, tc), lambda i, j: (i, j)),
                  pl.BlockSpec((tt, tc), lambda i, j: (i, j + nj)),
                  pl.BlockSpec((8, tc), halo),
                  pl.BlockSpec((8, tc), halo_v),
                  pl.BlockSpec((kw, tc), lambda i, j: (0, j)),
                  pl.BlockSpec((kw, tc), lambda i, j: (0, j + nj)),
                  pl.BlockSpec((1, tc), lambda i, j: (0, j)),
                  pl.BlockSpec((1, tc), lambda i, j: (0, j + nj))],
        out_specs=pl.BlockSpec((tt, tc), lambda i, j: (i, j)),
        out_shape=jax.ShapeDtypeStruct((T, F), BF16),
        scratch_shapes=[pltpu.VMEM((tt + 8, tc), F32), pltpu.VMEM((tt + 8, tc), F32)],
        compiler_params=_cparams(("parallel", "parallel")),
        name="ffn_act",
    )(u, u, u, u, cw, cw, cb.reshape(1, F2), cb.reshape(1, F2))


def _lru_kernel(xa_ref, ga_ref, cw_ref, cb_ref, wr_ref, br_ref, wi_ref, bi_ref, lam_ref, o_ref,
                xpad, a_s, u_s, hc, *, tt, kw):
    W = xa_ref.shape[1]

    @pl.when(pl.program_id(1) == 0)
    def _():
        xpad[0:8, :] = jnp.zeros((8, W), F32)
        hc[...] = jnp.zeros_like(hc)

    xpad[8:8 + tt, :] = xa_ref[...]
    xc = cb_ref[...]
    for k in range(kw):
        xc = xc + cw_ref[k:k + 1, :] * xpad[pl.ds(8 - (kw - 1) + k, tt), :]
    xpad[0:8, :] = xpad[tt:tt + 8, :]

    xb = xc.astype(BF16)
    nt = W // V7X_MXU_DIM

    def gate(w_ref, b_ref):
        parts = [jnp.dot(xb[:, j * V7X_MXU_DIM:(j + 1) * V7X_MXU_DIM], w_ref[j],
                         preferred_element_type=F32) for j in range(nt)]
        return _sigmoid(jnp.concatenate(parts, axis=1) + b_ref[...])

    r = gate(wr_ref, br_ref)
    gi = gate(wi_ref, bi_ref)
    nl = -lam_ref[...]
    sp = jnp.maximum(nl, 0.0) + jnp.log1p(jnp.exp(-jnp.abs(nl)))
    log_a = (-LRU_C) * r * sp
    a_s[...] = jnp.exp(log_a)
    th = jnp.tanh(log_a)
    u_s[...] = jnp.sqrt(-2.0 * th / (1.0 - th)) * (gi * xc)

    rows = lax.broadcasted_iota(jnp.int32, (8, W), 0)

    def body(i, h):
        sl = pl.ds(pl.multiple_of(i * 8, 8), 8)
        a = a_s[sl, :]
        u = u_s[sl, :]
        for d in (1, 2, 4):
            m = rows >= d
            a_sh = pltpu.roll(a, d, axis=0)
            u_sh = pltpu.roll(u, d, axis=0)
            u = jnp.where(m, a * u_sh + u, u)
            a = jnp.where(m, a * a_sh, a)
        hh = a * h + u
        a_s[sl, :] = hh
        return hh[7:8, :]

    hc[0:1, :] = lax.fori_loop(0, tt // 8, body, hc[0:1, :])
    o_ref[...] = (a_s[...] * _gelu_tanh(ga_ref[...])).astype(o_ref.dtype)


def _blockdiag_tiles(w):
    nb, k, _ = w.shape
    per = V7X_MXU_DIM // k
    w4 = w.reshape(nb // per, per, k, k)
    eye = jnp.eye(per, dtype=w.dtype)
    t = jnp.einsum('tbij,bc->tbicj', w4, eye)
    return t.reshape(nb // per, V7X_MXU_DIM, V7X_MXU_DIM)


def lru_mixer(proj, cw, cb, w_r, b_r, w_i, b_i, lam, *, batch, seq, tt=256):
    T = proj.shape[0]
    kw, W = cw.shape
    tt = min(tt, seq)
    nt = seq // tt
    wr = _blockdiag_tiles(w_r).astype(BF16)
    wi = _blockdiag_tiles(w_i).astype(BF16)
    vec = lambda b, t: (0, 0)
    wspec = pl.BlockSpec(wr.shape, lambda b, t: (0, 0, 0))
    return pl.pallas_call(
        functools.partial(_lru_kernel, tt=tt, kw=kw),
        grid=(batch, nt),
        in_specs=[pl.BlockSpec((tt, W), lambda b, t: (b * nt + t, 0)),
                  pl.BlockSpec((tt, W), lambda b, t: (b * nt + t, 1)),
                  pl.BlockSpec((kw, W), vec), pl.BlockSpec((1, W), vec),
                  wspec, pl.BlockSpec((1, W), vec),
                  wspec, pl.BlockSpec((1, W), vec),
                  pl.BlockSpec((1, W), vec)],
        out_specs=pl.BlockSpec((tt, W), lambda b, t: (b * nt + t, 0)),
        out_shape=jax.ShapeDtypeStruct((T, W), BF16),
        scratch_shapes=[pltpu.VMEM((tt + 8, W), F32), pltpu.VMEM((tt, W), F32),
                        pltpu.VMEM((tt, W), F32), pltpu.VMEM((8, W), F32)],
        compiler_params=_cparams(("parallel", "arbitrary")),
        name="lru_mixer",
    )(proj, proj, cw, cb.reshape(1, W), wr, b_r.reshape(1, W), wi, b_i.reshape(1, W),
      lam.reshape(1, W))


def _hgrn2_kernel(q_ref, f_ref, v_ref, g_ref, lb_ref, nw_ref, o_ref, st, *, tt, layer):
    C = HG_CHUNK

    @pl.when(pl.program_id(2) == 0)
    def _():
        st[...] = jnp.zeros_like(st)

    raw = lb_ref[...]
    e = jnp.exp(raw - jnp.max(raw, axis=0, keepdims=True))
    p = e / jnp.sum(e, axis=0, keepdims=True)
    lb = -p[0:1, :]
    for i in range(layer + 1):
        lb = lb + p[i:i + 1, :]

    r = lax.broadcasted_iota(jnp.int32, (C, C), 0)
    c = lax.broadcasted_iota(jnp.int32, (C, C), 1)
    causal = r >= c
    scale = HG_DK ** -0.5

    def chunk(ci, carry):
        sl = pl.ds(pl.multiple_of(ci * C, C), C)
        f = lb + (1.0 - lb) * _sigmoid(f_ref[sl, :])
        k = 1.0 - f
        cum = _cumsum_rows(jnp.log(f))
        tot = cum[C - 1:C, :]
        mid = cum[C // 2 - 1:C // 2, :]
        qs = _silu(q_ref[sl, :]) * scale
        vb = v_ref[sl, :].astype(BF16)
        qd = (qs * jnp.exp(cum - mid)).astype(BF16)
        kd = (k * jnp.exp(mid - cum)).astype(BF16)
        sc = lax.dot_general(qd, kd, (((1,), (1,)), ((), ())), preferred_element_type=F32)
        sc = jnp.where(causal, sc, 0.0).astype(BF16)
        s_old = st[...]
        o = jnp.dot(sc, vb, preferred_element_type=F32)
        o = o + lax.dot_general((qs * jnp.exp(cum)).astype(BF16), s_old.astype(BF16),
                                (((1,), (1,)), ((), ())), preferred_element_type=F32)
        kdec = (k * jnp.exp(tot - cum)).astype(BF16)
        upd = lax.dot_general(vb, kdec, (((0,), (0,)), ((), ())), preferred_element_type=F32)
        st[...] = jnp.exp(tot) * s_old + upd
        o = o * lax.rsqrt(jnp.mean(o * o, axis=-1, keepdims=True) + EPS) * nw_ref[...]
        o_ref[sl, :] = (o * _silu(g_ref[sl, :])).astype(o_ref.dtype)
        return carry

    lax.fori_loop(0, tt // C, chunk, 0)


def hgrn2_mixer(proj, lower_bounds, norm_w, *, layer, batch, seq, col0, tt=512):
    T = proj.shape[0]
    tt = min(tt, seq)
    nt = seq // tt
    H = HG_HEADS
    c0 = col0 // HG_DK

    def col(k):
        return pl.BlockSpec((tt, HG_DK), lambda b, h, t: (b * nt + t, c0 + k * H + h))

    ne = lower_bounds.shape[0]
    return pl.pallas_call(
        functools.partial(_hgrn2_kernel, tt=tt, layer=layer),
        grid=(batch, H, nt),
        in_specs=[col(0), col(1), col(2), col(3),
                  pl.BlockSpec((ne, HG_DK), lambda b, h, t: (0, h)),
                  pl.BlockSpec((1, HG_DK), lambda b, h, t: (0, 0))],
        out_specs=pl.BlockSpec((tt, HG_DK), lambda b, h, t: (b * nt + t, h)),
        out_shape=jax.ShapeDtypeStruct((T, H * HG_DK), BF16),
        scratch_shapes=[pltpu.VMEM((HG_DK, HG_DK), F32)],
        compiler_params=_cparams(("parallel", "parallel", "arbitrary")),
        name="hgrn2_mixer",
    )(proj, proj, proj, proj, lower_bounds, norm_w.reshape(1, HG_DK))


def _ssd_prep_kernel(dt_ref, bias_ref, alog_ref, cumg_ref, cumt_ref, dtt_ref, wt_ref, etb_ref):
    L = SSD_CHUNK
    x = dt_ref[...] + bias_ref[...]
    dt = jnp.maximum(x, 0.0) + jnp.log1p(jnp.exp(-jnp.abs(x)))
    a_neg = -jnp.exp(alog_ref[...])
    cum = _cumsum_rows(dt * a_neg)
    for g in range(SSD_GROUPS):
        sh = (V7X_LANES - SSD_HPG * g) % V7X_LANES
        cumg_ref[:, g * V7X_LANES:(g + 1) * V7X_LANES] = cum if sh == 0 else pltpu.roll(cum, sh, axis=1)
    cum_t = cum.T
    dt_t = dt.T
    tot = jnp.broadcast_to(cum_t[:, L - 1:L], cum_t.shape)
    cumt_ref[0] = cum_t
    dtt_ref[0] = dt_t
    wt_ref[0] = dt_t * jnp.exp(tot - cum_t)
    etb_ref[0] = jnp.exp(tot)


def ssd_prep(proj, dt_bias, a_log, *, dt_col):
    T = proj.shape[0]
    L = SSD_CHUNK
    nc = T // L
    nh = dt_bias.shape[0]
    pad = V7X_LANES - nh
    bias = jnp.pad(dt_bias, (0, pad)).reshape(1, V7X_LANES)
    alog = jnp.pad(a_log, (0, pad)).reshape(1, V7X_LANES)
    sq = jax.ShapeDtypeStruct((nc, V7X_LANES, L), F32)
    sqspec = pl.BlockSpec((1, V7X_LANES, L), lambda i: (i, 0, 0))
    return pl.pallas_call(
        _ssd_prep_kernel,
        grid=(nc,),
        in_specs=[pl.BlockSpec((L, V7X_LANES), lambda i: (i, dt_col // V7X_LANES)),
                  pl.BlockSpec((1, V7X_LANES), lambda i: (0, 0)),
                  pl.BlockSpec((1, V7X_LANES), lambda i: (0, 0))],
        out_specs=[pl.BlockSpec((L, SSD_GROUPS * V7X_LANES), lambda i: (i, 0)),
                   sqspec, sqspec, sqspec, sqspec],
        out_shape=[jax.ShapeDtypeStruct((T, SSD_GROUPS * V7X_LANES), F32), sq, sq, sq, sq],
        compiler_params=_cparams(("parallel",)),
        name="ssd_prep",
    )(proj, bias, alog)


def _ssd_kernel(z_ref, x_ref, b_ref, c_ref, cwx_ref, cwb_ref, cwc_ref, cbx_ref, cbb_ref, cbc_ref,
                cumg_ref, cumt_ref, dtt_ref, wt_ref, etb_ref, d_ref, nw_ref, o_ref,
                xpad, bpad, cpad, xc, bc, cc, st, *, tt, kw):
    L = SSD_CHUNK
    P2 = 2 * SSD_HEADDIM

    @pl.when(pl.program_id(2) == 0)
    def _():
        xpad[0:8, :] = jnp.zeros((8, xpad.shape[1]), F32)
        bpad[0:8, :] = jnp.zeros((8, bpad.shape[1]), F32)
        cpad[0:8, :] = jnp.zeros((8, cpad.shape[1]), F32)
        st[...] = jnp.zeros_like(st)

    def conv(src, pad, w_ref, bias_ref, dst):
        pad[8:8 + tt, :] = src[...]
        y = bias_ref[...]
        for k in range(kw):
            y = y + w_ref[k:k + 1, :] * pad[pl.ds(8 - (kw - 1) + k, tt), :]
        pad[0:8, :] = pad[tt:tt + 8, :]
        dst[...] = _silu(y)

    conv(x_ref, xpad, cwx_ref, cbx_ref, xc)
    conv(b_ref, bpad, cwb_ref, cbb_ref, bc)
    conv(c_ref, cpad, cwc_ref, cbc_ref, cc)

    r = lax.broadcasted_iota(jnp.int32, (L, L), 0)
    c = lax.broadcasted_iota(jnp.int32, (L, L), 1)
    causal = r >= c
    lo = lax.broadcasted_iota(jnp.int32, (L, P2), 1) < SSD_HEADDIM
    lo_row = lo[0:1, :]

    def chunk(ci, carry):
        sl = pl.ds(pl.multiple_of(ci * L, L), L)
        x = xc[sl, :]
        bf = bc[sl, :]
        cb16 = cc[sl, :].astype(BF16)
        b_t = bf.T
        cbm = lax.dot_general(cb16, bf.astype(BF16), (((1,), (1,)), ((), ())),
                              preferred_element_type=F32)
        s_old = st[...]
        yoff = jnp.dot(cb16, s_old.astype(BF16), preferred_element_type=F32)
        cumg = cumg_ref[sl, :]
        cum_t = cumt_ref[ci]
        dt_t = dtt_ref[ci]
        w_t = wt_ref[ci]
        et_b = etb_ref[ci]
        ys = []
        for p in range(SSD_HPG // 2):
            ms, ws, cols = [], [], []
            for j in (2 * p, 2 * p + 1):
                col = jnp.broadcast_to(cumg[:, j:j + 1], (L, L))
                seg = jnp.where(causal, col - cum_t[j:j + 1, :], NEG_BIG)
                ms.append(cbm * jnp.exp(seg) * dt_t[j:j + 1, :])
                ws.append(b_t * w_t[j:j + 1, :])
                cols.append(col)
            lhs = jnp.concatenate([jnp.concatenate(ms, axis=1), jnp.concatenate(ws, axis=1)],
                                  axis=0).astype(BF16)
            xp = x[:, p * P2:(p + 1) * P2]
            x2 = jnp.concatenate([jnp.where(lo, xp, 0.0), jnp.where(lo, 0.0, xp)],
                                 axis=0).astype(BF16)
            res = jnp.dot(lhs, x2, preferred_element_type=F32)
            ecum = jnp.exp(jnp.where(lo, cols[0], cols[1]))
            ys.append(res[:L] + yoff[:, p * P2:(p + 1) * P2] * ecum)
            et = jnp.where(lo_row, et_b[2 * p:2 * p + 1, :], et_b[2 * p + 1:2 * p + 2, :])
            st[:, p * P2:(p + 1) * P2] = et * s_old[:, p * P2:(p + 1) * P2] + res[L:]
        y = jnp.concatenate(ys, axis=1) + x * d_ref[...]
        y = y * _silu(z_ref[sl, :])
        y = y * lax.rsqrt(jnp.mean(y * y, axis=-1, keepdims=True) + EPS) * nw_ref[...]
        o_ref[sl, :] = y.astype(o_ref.dtype)
        return carry

    lax.fori_loop(0, tt // L, chunk, 0)


def ssd_mixer(proj, prep, conv_w, conv_b, d_skip, norm_w, *, batch, seq, tt=512):
    T = proj.shape[0]
    cumg, cumt, dtt, wt, etb = prep
    kw = conv_w.shape[0]
    G = SSD_GROUPS
    gw = SSD_HPG * SSD_HEADDIM
    inner = G * gw
    N = SSD_STATE
    L = SSD_CHUNK
    tt = min(tt, seq)
    nt = seq // tt
    ncb = tt // L
    row = lambda b, g, t: b * nt + t
    xo, bo, co = inner // gw, 2 * inner // N, (2 * inner + G * N) // N
    d_exp = jnp.repeat(d_skip, SSD_HEADDIM).reshape(1, inner)
    cbias = conv_b.reshape(1, -1)
    sq = lambda: pl.BlockSpec((ncb, SSD_HPG, L), lambda b, g, t: (row(b, g, t), g, 0))
    return pl.pallas_call(
        functools.partial(_ssd_kernel, tt=tt, kw=kw),
        grid=(batch, G, nt),
        in_specs=[pl.BlockSpec((tt, gw), lambda b, g, t: (row(b, g, t), g)),
                  pl.BlockSpec((tt, gw), lambda b, g, t: (row(b, g, t), xo + g)),
                  pl.BlockSpec((tt, N), lambda b, g, t: (row(b, g, t), bo + g)),
                  pl.BlockSpec((tt, N), lambda b, g, t: (row(b, g, t), co + g)),
                  pl.BlockSpec((kw, gw), lambda b, g, t: (0, g)),
                  pl.BlockSpec((kw, N), lambda b, g, t: (0, inner // N + g)),
                  pl.BlockSpec((kw, N), lambda b, g, t: (0, inner // N + G + g)),
                  pl.BlockSpec((1, gw), lambda b, g, t: (0, g)),
                  pl.BlockSpec((1, N), lambda b, g, t: (0, inner // N + g)),
                  pl.BlockSpec((1, N), lambda b, g, t: (0, inner // N + G + g)),
                  pl.BlockSpec((tt, V7X_LANES), lambda b, g, t: (row(b, g, t), g)),
                  sq(), sq(), sq(), sq(),
                  pl.BlockSpec((1, gw), lambda b, g, t: (0, g)),
                  pl.BlockSpec((1, gw), lambda b, g, t: (0, g))],
        out_specs=pl.BlockSpec((tt, gw), lambda b, g, t: (row(b, g, t), g)),
        out_shape=jax.ShapeDtypeStruct((T, inner), BF16),
        scratch_shapes=[pltpu.VMEM((tt + 8, gw), F32), pltpu.VMEM((tt + 8, N), F32),
                        pltpu.VMEM((tt + 8, N), F32), pltpu.VMEM((tt, gw), F32),
                        pltpu.VMEM((tt, N), F32), pltpu.VMEM((tt, N), F32),
                        pltpu.VMEM((N, gw), F32)],
        compiler_params=_cparams(("parallel", "parallel", "arbitrary")),
        name="ssd_mixer",
    )(proj, proj, proj, proj, conv_w, conv_w, conv_w, cbias, cbias, cbias,
      cumg, cumt, dtt, wt, etb, d_exp, norm_w.reshape(1, inner))


def _rmsnorm_kernel(x_ref, w_ref, o_ref):
    x = x_ref[...]
    ms = jnp.mean(x * x, axis=-1, keepdims=True)
    o_ref[...] = x * lax.rsqrt(ms + EPS) * w_ref[...]


def rmsnorm(x, w, *, tm=256):
    T, D = x.shape
    return pl.pallas_call(
        _rmsnorm_kernel,
        grid=(T // tm,),
        in_specs=[pl.BlockSpec((tm, D), lambda i: (i, 0)), pl.BlockSpec((1, D), lambda i: (0, 0))],
        out_specs=pl.BlockSpec((tm, D), lambda i: (i, 0)),
        out_shape=jax.ShapeDtypeStruct((T, D), F32),
        compiler_params=_cparams(("parallel",)),
        name="final_rmsnorm",
    )(x, w.reshape(1, D))


def even_layer(x, nw, w_in, lru_cw, lru_cb, w_r, b_r, w_i, b_i, lam, lower_bounds, hg_nw, w_out,
               *, layer, batch, seq):
    lru_w = lru_cw.shape[1]
    proj = norm_matmul(x, nw, w_in.astype(BF16), tm=1024, tn=1024)
    ya = lru_mixer(proj, lru_cw, lru_cb, w_r, b_r, w_i, b_i, lam, batch=batch, seq=seq)
    yb = hgrn2_mixer(proj, lower_bounds, hg_nw, layer=layer, batch=batch, seq=seq, col0=2 * lru_w)
    return matmul_residual([ya, yb], w_out.astype(BF16), x, tm=512, tn=w_out.shape[1])


def odd_layer(x, nw, w_in, conv_w, conv_b, dt_bias, a_log, d_skip, norm_w, w_out, *, batch, seq):
    D, n_in = w_in.shape
    nh = dt_bias.shape[0]
    dt_col = n_in - nh
    tn = 1152
    n_pad = -(-n_in // tn) * tn
    w = jnp.pad(w_in, ((0, 0), (0, n_pad - n_in))).astype(BF16)
    proj = norm_matmul(x, nw, w, tm=1024, tn=tn)
    prep = ssd_prep(proj, dt_bias, a_log, dt_col=dt_col)
    y = ssd_mixer(proj, prep, conv_w, conv_b, d_skip, norm_w, batch=batch, seq=seq)
    return matmul_residual([y], w_out.astype(BF16), x, tm=512, tn=1024)


def ffn_layer(x, nw, w_up, conv_w, conv_b, w_down, *, seq):
    u = norm_matmul(x, nw, w_up.astype(BF16), tm=1024, tn=1024)
    act = ffn_act(u, conv_w, conv_b, seq=seq)
    return matmul_residual([act], w_down.astype(BF16), x, tm=512, tn=1024)


def kernel(x, norm_mix_w, norm_ffn_w, norm_final_w, ev_w_in, lru_conv_w, lru_conv_b, lru_w_r, lru_b_r,
           lru_w_i, lru_b_i, lru_lambda, hg_lower_bounds, hg_norm_w, ev_w_out, ssd_w_in, ssd_conv_w,
           ssd_conv_b, ssd_dt_bias, ssd_a_log, ssd_d, ssd_norm_w, ssd_w_out, ffn_w_up, ffn_conv_w,
           ffn_conv_b, ffn_w_down):
    batch, seq, d = x.shape
    depth = norm_mix_w.shape[0]
    h = x.reshape(batch * seq, d)
    for l in range(depth):
        if l % 2 == 0:
            e = l // 2
            h = even_layer(h, norm_mix_w[l], ev_w_in[e], lru_conv_w[e], lru_conv_b[e], lru_w_r[e],
                           lru_b_r[e], lru_w_i[e], lru_b_i[e], lru_lambda[e], hg_lower_bounds,
                           hg_norm_w[e], ev_w_out[e], layer=e, batch=batch, seq=seq)
        else:
            o = l // 2
            h = odd_layer(h, norm_mix_w[l], ssd_w_in[o], ssd_conv_w[o], ssd_conv_b[o], ssd_dt_bias[o],
                          ssd_a_log[o], ssd_d[o], ssd_norm_w[o], ssd_w_out[o], batch=batch, seq=seq)
        h = ffn_layer(h, norm_ffn_w[l], ffn_w_up[l], ffn_conv_w[l], ffn_conv_b[l], ffn_w_down[l], seq=seq)
    return rmsnorm(h, norm_final_w).reshape(batch, seq, d)
```

```python
import functools

import jax
import jax.numpy as jnp
from jax import lax
from jax.experimental import pallas as pl
from jax.experimental.pallas import tpu as pltpu

F32 = jnp.float32
BF16 = jnp.bfloat16
EPS = 1e-6

V7X_LANES = 128
V7X_SUBLANES = 8
V7X_MXU_DIM = 256
V7X_VMEM_BYTES = 64 * 1024 * 1024
VMEM_LIMIT = V7X_VMEM_BYTES - 8 * 1024 * 1024

LRU_BLOCKS = 16
LRU_C = 8.0
HG_HEADS = 8
HG_DK = 128
HG_CHUNK = 64
SSD_HEADDIM = 64
SSD_GROUPS = 8
SSD_HPG = 8
SSD_STATE = 128
SSD_CHUNK = 128
NEG_BIG = -1e30


def _cparams(sem):
    return pltpu.CompilerParams(dimension_semantics=sem, vmem_limit_bytes=VMEM_LIMIT)


def _sigmoid(x):
    return 1.0 / (1.0 + jnp.exp(-x))


def _silu(x):
    return x * _sigmoid(x)


def _gelu_tanh(x):
    return 0.5 * x * (1.0 + jnp.tanh(0.7978845608028654 * (x + 0.044715 * (x * x * x))))


def _split3(x):
    hi = x.astype(BF16)
    r1 = x - hi.astype(F32)
    mid = r1.astype(BF16)
    lo = (r1 - mid.astype(F32)).astype(BF16)
    return jnp.concatenate([hi, mid, lo], axis=1)


def _cumsum_rows(x):
    n, w = x.shape
    r = lax.broadcasted_iota(jnp.int32, (n, n), 0)
    c = lax.broadcasted_iota(jnp.int32, (n, n), 1)
    tril = jnp.where(r >= c, 1.0, 0.0).astype(BF16)
    p = jnp.dot(tril, _split3(x), preferred_element_type=F32)
    return p[:, :w] + p[:, w:2 * w] + p[:, 2 * w:]


def _norm_mm_kernel(x_ref, nw_ref, w_ref, o_ref, h_ref, *, rows):
    @pl.when(pl.program_id(1) == 0)
    def _():
        def body(r, c):
            sl = pl.ds(pl.multiple_of(r * rows, rows), rows)
            x = x_ref[sl, :]
            ms = jnp.mean(x * x, axis=-1, keepdims=True)
            h_ref[sl, :] = (x * lax.rsqrt(ms + EPS) * nw_ref[...]).astype(BF16)
            return c
        lax.fori_loop(0, x_ref.shape[0] // rows, body, 0)

    o_ref[...] = jnp.dot(h_ref[...], w_ref[...], preferred_element_type=F32).astype(o_ref.dtype)


def norm_matmul(x, nw, w, *, tm, tn, out_dtype=F32):
    T, D = x.shape
    N = w.shape[1]
    tm = min(tm, T)
    return pl.pallas_call(
        functools.partial(_norm_mm_kernel, rows=128),
        grid=(T // tm, N // tn),
        in_specs=[pl.BlockSpec((tm, D), lambda i, j: (i, 0)),
                  pl.BlockSpec((1, D), lambda i, j: (0, 0)),
                  pl.BlockSpec((D, tn), lambda i, j: (0, j))],
        out_specs=pl.BlockSpec((tm, tn), lambda i, j: (i, j)),
        out_shape=jax.ShapeDtypeStruct((T, N), out_dtype),
        scratch_shapes=[pltpu.VMEM((tm, D), BF16)],
        compiler_params=_cparams(("parallel", "arbitrary")),
        name="norm_matmul",
    )(x, nw.reshape(1, D), w)


def _mm_res_kernel(*refs, nl):
    lhs = refs[:nl]
    w_ref, r_ref, o_ref = refs[nl], refs[nl + 1], refs[nl + 2]
    acc = r_ref[...]
    k0 = 0
    for l in lhs:
        k = l.shape[1]
        acc = acc + jnp.dot(l[...], w_ref[k0:k0 + k, :], preferred_element_type=F32)
        k0 += k
    o_ref[...] = acc


def matmul_residual(lhs_list, w, res, *, tm, tn):
    T, N = res.shape
    K = w.shape[0]
    tm = min(tm, T)
    nl = len(lhs_list)
    in_specs = [pl.BlockSpec((tm, l.shape[1]), lambda j, i: (i, 0)) for l in lhs_list]
    in_specs += [pl.BlockSpec((K, tn), lambda j, i: (0, j)),
                 pl.BlockSpec((tm, tn), lambda j, i: (i, j))]
    return pl.pallas_call(
        functools.partial(_mm_res_kernel, nl=nl),
        grid=(N // tn, T // tm),
        in_specs=in_specs,
        out_specs=pl.BlockSpec((tm, tn), lambda j, i: (i, j)),
        out_shape=jax.ShapeDtypeStruct((T, N), F32),
        compiler_params=_cparams(("parallel", "parallel")),
        name="matmul_residual",
    )(*lhs_list, w, res)


def _ffn_up_kernel(x_ref, nw_ref, wg_ref, wv_ref, cwg_ref, cwv_ref, cbg_ref, cbv_ref, o_ref,
                   h_ref, gpad, vpad, halo, *, tm, seq, kw, rows, sub):
    i = pl.program_id(0)
    j = pl.program_id(1)

    @pl.when(j == 0)
    def _():
        def body(r, c):
            sl = pl.ds(pl.multiple_of(r * rows, rows), rows)
            x = x_ref[sl, :]
            ms = jnp.mean(x * x, axis=-1, keepdims=True)
            h_ref[sl, :] = (x * lax.rsqrt(ms + EPS) * nw_ref[...]).astype(BF16)
            return c
        lax.fori_loop(0, tm // rows, body, 0)

    first = (i * tm) % seq == 0
    tc = o_ref.shape[1]

    def branch(w_ref, pad, cw_ref, cb_ref, slot, s):
        cs = slice(s * sub, (s + 1) * sub)
        pad[s, 0:8, :] = jnp.where(first, 0.0, halo[j, slot, :, cs])
        pad[s, 8:8 + tm, :] = jnp.dot(h_ref[...], w_ref[:, cs], preferred_element_type=F32)
        halo[j, slot, :, cs] = pad[s, tm:tm + 8, :]
        y = cb_ref[:, cs] + cw_ref[kw - 1:kw, cs] * pad[s, 8:8 + tm, :]
        for k in range(kw - 1):
            y = y + cw_ref[k:k + 1, cs] * pad[s, pl.ds(8 - (kw - 1) + k, tm), :]
        return y

    for s in range(tc // sub):
        g = branch(wg_ref, gpad, cwg_ref, cbg_ref, 0, s)
        v = branch(wv_ref, vpad, cwv_ref, cbv_ref, 1, s)
        o_ref[:, s * sub:(s + 1) * sub] = (_silu(g) * v).astype(o_ref.dtype)


def ffn_up(x, nw, w_up, cw, cb, *, seq, tm=1024, tc=512, sub=256):
    T, D = x.shape
    F2 = w_up.shape[1]
    F = F2 // 2
    kw = cw.shape[0]
    tm = min(tm, seq)
    nj = F // tc
    cbr = cb.reshape(1, F2)
    return pl.pallas_call(
        functools.partial(_ffn_up_kernel, tm=tm, seq=seq, kw=kw, rows=128, sub=sub),
        grid=(T // tm, nj),
        in_specs=[pl.BlockSpec((tm, D), lambda i, j: (i, 0)),
                  pl.BlockSpec((1, D), lambda i, j: (0, 0)),
                  pl.BlockSpec((D, tc), lambda i, j: (0, j)),
                  pl.BlockSpec((D, tc), lambda i, j: (0, j + nj)),
                  pl.BlockSpec((kw, tc), lambda i, j: (0, j)),
                  pl.BlockSpec((kw, tc), lambda i, j: (0, j + nj)),
                  pl.BlockSpec((1, tc), lambda i, j: (0, j)),
                  pl.BlockSpec((1, tc), lambda i, j: (0, j + nj))],
        out_specs=pl.BlockSpec((tm, tc), lambda i, j: (i, j)),
        out_shape=jax.ShapeDtypeStruct((T, F), BF16),
        scratch_shapes=[pltpu.VMEM((tm, D), BF16),
                        pltpu.VMEM((tc // sub, tm + 8, sub), F32),
                        pltpu.VMEM((tc // sub, tm + 8, sub), F32),
                        pltpu.VMEM((nj, 2, 8, tc), F32)],
        compiler_params=_cparams(("arbitrary", "arbitrary")),
        name="ffn_up",
    )(x, nw.reshape(1, D), w_up, w_up, cw, cw, cbr, cbr)


def _lru_kernel(xa_ref, ga_ref, cw_ref, cb_ref, wr_ref, br_ref, wi_ref, bi_ref, lam_ref, o_ref,
                xpad, a_s, u_s, hc, *, tt, kw):
    W = xa_ref.shape[1]

    @pl.when(pl.program_id(1) == 0)
    def _():
        xpad[0:8, :] = jnp.zeros((8, W), F32)
        hc[...] = jnp.zeros_like(hc)

    xpad[8:8 + tt, :] = xa_ref[...]
    xc = cb_ref[...]
    for k in range(kw):
        xc = xc + cw_ref[k:k + 1, :] * xpad[pl.ds(8 - (kw - 1) + k, tt), :]
    xpad[0:8, :] = xpad[tt:tt + 8, :]

    xb = xc.astype(BF16)
    nt = W // V7X_MXU_DIM

    def gate(w_ref, b_ref):
        parts = [jnp.dot(xb[:, j * V7X_MXU_DIM:(j + 1) * V7X_MXU_DIM], w_ref[j],
                         preferred_element_type=F32) for j in range(nt)]
        return _sigmoid(jnp.concatenate(parts, axis=1) + b_ref[...])

    r = gate(wr_ref, br_ref)
    gi = gate(wi_ref, bi_ref)
    nl = -lam_ref[...]
    sp = jnp.maximum(nl, 0.0) + jnp.log1p(jnp.exp(-jnp.abs(nl)))
    log_a = (-LRU_C) * r * sp
    a_s[...] = jnp.exp(log_a)
    th = jnp.tanh(log_a)
    u_s[...] = jnp.sqrt(-2.0 * th / (1.0 - th)) * (gi * xc)

    rows = lax.broadcasted_iota(jnp.int32, (8, W), 0)

    def body(i, h):
        sl = pl.ds(pl.multiple_of(i * 8, 8), 8)
        a = a_s[sl, :]
        u = u_s[sl, :]
        for d in (1, 2, 4):
            m = rows >= d
            a_sh = pltpu.roll(a, d, axis=0)
            u_sh = pltpu.roll(u, d, axis=0)
            u = jnp.where(m, a * u_sh + u, u)
            a = jnp.where(m, a * a_sh, a)
        hh = a * h + u
        a_s[sl, :] = hh
        return hh[7:8, :]

    hc[0:1, :] = lax.fori_loop(0, tt // 8, body, hc[0:1, :])
    o_ref[...] = (a_s[...] * _gelu_tanh(ga_ref[...])).astype(o_ref.dtype)


def _blockdiag_tiles(w):
    nb, k, _ = w.shape
    per = V7X_MXU_DIM // k
    w4 = w.reshape(nb // per, per, k, k)
    eye = jnp.eye(per, dtype=w.dtype)
    t = jnp.einsum('tbij,bc->tbicj', w4, eye)
    return t.reshape(nb // per, V7X_MXU_DIM, V7X_MXU_DIM)


def lru_mixer(proj, cw, cb, w_r, b_r, w_i, b_i, lam, *, batch, seq, tt=256):
    T = proj.shape[0]
    kw, W = cw.shape
    tt = min(tt, seq)
    nt = seq // tt
    wr = _blockdiag_tiles(w_r).astype(BF16)
    wi = _blockdiag_tiles(w_i).astype(BF16)
    vec = lambda b, t: (0, 0)
    wspec = pl.BlockSpec(wr.shape, lambda b, t: (0, 0, 0))
    return pl.pallas_call(
        functools.partial(_lru_kernel, tt=tt, kw=kw),
        grid=(batch, nt),
        in_specs=[pl.BlockSpec((tt, W), lambda b, t: (b * nt + t, 0)),
                  pl.BlockSpec((tt, W), lambda b, t: (b * nt + t, 1)),
                  pl.BlockSpec((kw, W), vec), pl.BlockSpec((1, W), vec),
                  wspec, pl.BlockSpec((1, W), vec),
                  wspec, pl.BlockSpec((1, W), vec),
                  pl.BlockSpec((1, W), vec)],
        out_specs=pl.BlockSpec((tt, W), lambda b, t: (b * nt + t, 0)),
        out_shape=jax.ShapeDtypeStruct((T, W), BF16),
        scratch_shapes=[pltpu.VMEM((tt + 8, W), F32), pltpu.VMEM((tt, W), F32),
                        pltpu.VMEM((tt, W), F32), pltpu.VMEM((8, W), F32)],
        compiler_params=_cparams(("parallel", "arbitrary")),
        name="lru_mixer",
    )(proj, proj, cw, cb.reshape(1, W), wr, b_r.reshape(1, W), wi, b_i.reshape(1, W),
      lam.reshape(1, W))


def _hgrn2_kernel(q_ref, f_ref, v_ref, g_ref, lb_ref, nw_ref, o_ref, st, *, tt, layer, hb):
    C = HG_CHUNK

    @pl.when(pl.program_id(2) == 0)
    def _():
        st[...] = jnp.zeros_like(st)

    raw = lb_ref[...]
    e = jnp.exp(raw - jnp.max(raw, axis=0, keepdims=True))
    p = e / jnp.sum(e, axis=0, keepdims=True)
    lb_all = -p[0:1, :]
    for i in range(layer + 1):
        lb_all = lb_all + p[i:i + 1, :]

    r = lax.broadcasted_iota(jnp.int32, (C, C), 0)
    c = lax.broadcasted_iota(jnp.int32, (C, C), 1)
    causal = r >= c
    scale = HG_DK ** -0.5
    nw = nw_ref[...]

    nt_dims = (((1,), (1,)), ((), ()))
    tn_dims = (((0,), (0,)), ((), ()))
    heads = range(hb)
    cols = [slice(h * HG_DK, (h + 1) * HG_DK) for h in heads]

    def chunk(ci, carry):
        sl = pl.ds(pl.multiple_of(ci * C, C), C)
        ks, cums = [], []
        for h in heads:
            lb = lb_all[:, cols[h]]
            f = lb + (1.0 - lb) * _sigmoid(f_ref[sl, cols[h]])
            ks.append(1.0 - f)
            cums.append(_cumsum_rows(jnp.log(f)))
        qss, scs = [], []
        for h in heads:
            cum, k = cums[h], ks[h]
            mid = cum[C // 2 - 1:C // 2, :]
            qs = _silu(q_ref[sl, cols[h]]) * scale
            qd = (qs * jnp.exp(cum - mid)).astype(BF16)
            kd = (k * jnp.exp(mid - cum)).astype(BF16)
            qss.append(qs)
            scs.append(lax.dot_general(qd, kd, nt_dims, preferred_element_type=F32))
        outs = []
        for h in heads:
            cum, k = cums[h], ks[h]
            tot = cum[C - 1:C, :]
            vb = v_ref[sl, cols[h]].astype(BF16)
            sc = jnp.where(causal, scs[h], 0.0).astype(BF16)
            s_old = st[h]
            o = jnp.dot(sc, vb, preferred_element_type=F32)
            o = o + lax.dot_general((qss[h] * jnp.exp(cum)).astype(BF16), s_old.astype(BF16),
                                    nt_dims, preferred_element_type=F32)
            kdec = (k * jnp.exp(tot - cum)).astype(BF16)
            upd = lax.dot_general(vb, kdec, tn_dims, preferred_element_type=F32)
            st[h] = jnp.exp(tot) * s_old + upd
            outs.append(o)
        for h in heads:
            o = outs[h]
            o = o * lax.rsqrt(jnp.mean(o * o, axis=-1, keepdims=True) + EPS) * nw
            o_ref[sl, cols[h]] = (o * _silu(g_ref[sl, cols[h]])).astype(o_ref.dtype)
        return carry

    lax.fori_loop(0, tt // C, chunk, 0)


def hgrn2_mixer(proj, lower_bounds, norm_w, *, layer, batch, seq, col0, tt=256, hb=8):
    T = proj.shape[0]
    tt = min(tt, seq)
    nt = seq // tt
    H = HG_HEADS
    bw = hb * HG_DK
    c0 = col0 // bw
    nhb = H // hb

    def col(k):
        return pl.BlockSpec((tt, bw), lambda b, h, t: (b * nt + t, c0 + k * nhb + h))

    ne = lower_bounds.shape[0]
    return pl.pallas_call(
        functools.partial(_hgrn2_kernel, tt=tt, layer=layer, hb=hb),
        grid=(batch, nhb, nt),
        in_specs=[col(0), col(1), col(2), col(3),
                  pl.BlockSpec((ne, bw), lambda b, h, t: (0, h)),
                  pl.BlockSpec((1, HG_DK), lambda b, h, t: (0, 0))],
        out_specs=pl.BlockSpec((tt, bw), lambda b, h, t: (b * nt + t, h)),
        out_shape=jax.ShapeDtypeStruct((T, H * HG_DK), BF16),
        scratch_shapes=[pltpu.VMEM((hb, HG_DK, HG_DK), F32)],
        compiler_params=_cparams(("parallel", "parallel", "arbitrary")),
        name="hgrn2_mixer",
    )(proj, proj, proj, proj, lower_bounds, norm_w.reshape(1, HG_DK))


def _ssd_prep_kernel(dt_ref, bias_ref, alog_ref, cumg_ref, cumt_ref, dtt_ref, wt_ref, etb_ref):
    L = SSD_CHUNK
    x = dt_ref[...] + bias_ref[...]
    dt = jnp.maximum(x, 0.0) + jnp.log1p(jnp.exp(-jnp.abs(x)))
    a_neg = -jnp.exp(alog_ref[...])
    cum = _cumsum_rows(dt * a_neg)
    for g in range(SSD_GROUPS):
        sh = (V7X_LANES - SSD_HPG * g) % V7X_LANES
        cumg_ref[:, g * V7X_LANES:(g + 1) * V7X_LANES] = cum if sh == 0 else pltpu.roll(cum, sh, axis=1)
    cum_t = cum.T
    dt_t = dt.T
    tot = jnp.broadcast_to(cum_t[:, L - 1:L], cum_t.shape)
    cumt_ref[0] = cum_t
    dtt_ref[0] = dt_t
    wt_ref[0] = dt_t * jnp.exp(tot - cum_t)
    etb_ref[0] = jnp.exp(tot)


def ssd_prep(proj, dt_bias, a_log, *, dt_col):
    T = proj.shape[0]
    L = SSD_CHUNK
    nc = T // L
    nh = dt_bias.shape[0]
    pad = V7X_LANES - nh
    bias = jnp.pad(dt_bias, (0, pad)).reshape(1, V7X_LANES)
    alog = jnp.pad(a_log, (0, pad)).reshape(1, V7X_LANES)
    sq = jax.ShapeDtypeStruct((nc, V7X_LANES, L), F32)
    sqspec = pl.BlockSpec((1, V7X_LANES, L), lambda i: (i, 0, 0))
    return pl.pallas_call(
        _ssd_prep_kernel,
        grid=(nc,),
        in_specs=[pl.BlockSpec((L, V7X_LANES), lambda i: (i, dt_col // V7X_LANES)),
                  pl.BlockSpec((1, V7X_LANES), lambda i: (0, 0)),
                  pl.BlockSpec((1, V7X_LANES), lambda i: (0, 0))],
        out_specs=[pl.BlockSpec((L, SSD_GROUPS * V7X_LANES), lambda i: (i, 0)),
                   sqspec, sqspec, sqspec, sqspec],
        out_shape=[jax.ShapeDtypeStruct((T, SSD_GROUPS * V7X_LANES), F32), sq, sq, sq, sq],
        compiler_params=_cparams(("parallel",)),
        name="ssd_prep",
    )(proj, bias, alog)


def _ssd_kernel(z_ref, x_ref, b_ref, c_ref, cwx_ref, cwb_ref, cwc_ref, cbx_ref, cbb_ref, cbc_ref,
                cumg_ref, cumt_ref, dtt_ref, wt_ref, etb_ref, d_ref, nw_ref, o_ref,
                xpad, bpad, cpad, xc, bc, cc, st, *, tt, kw):
    L = SSD_CHUNK
    P2 = 2 * SSD_HEADDIM

    @pl.when(pl.program_id(2) == 0)
    def _():
        xpad[0:8, :] = jnp.zeros((8, xpad.shape[1]), F32)
        bpad[0:8, :] = jnp.zeros((8, bpad.shape[1]), F32)
        cpad[0:8, :] = jnp.zeros((8, cpad.shape[1]), F32)
        st[...] = jnp.zeros_like(st)

    def conv(src, pad, w_ref, bias_ref, dst):
        pad[8:8 + tt, :] = src[...]
        y = bias_ref[...]
        for k in range(kw):
            y = y + w_ref[k:k + 1, :] * pad[pl.ds(8 - (kw - 1) + k, tt), :]
        pad[0:8, :] = pad[tt:tt + 8, :]
        dst[...] = _silu(y)

    conv(x_ref, xpad, cwx_ref, cbx_ref, xc)
    conv(b_ref, bpad, cwb_ref, cbb_ref, bc)
    conv(c_ref, cpad, cwc_ref, cbc_ref, cc)

    r = lax.broadcasted_iota(jnp.int32, (L, L), 0)
    c = lax.broadcasted_iota(jnp.int32, (L, L), 1)
    causal = r >= c
    lo = lax.broadcasted_iota(jnp.int32, (L, P2), 1) < SSD_HEADDIM
    lo_row = lo[0:1, :]

    def chunk(ci, carry):
        sl = pl.ds(pl.multiple_of(ci * L, L), L)
        x = xc[sl, :]
        bf = bc[sl, :]
        cb16 = cc[sl, :].astype(BF16)
        b_t = bf.T
        cbm = lax.dot_general(cb16, bf.astype(BF16), (((1,), (1,)), ((), ())),
                              preferred_element_type=F32)
        s_old = st[...]
        yoff = jnp.dot(cb16, s_old.astype(BF16), preferred_element_type=F32)
        cumg = cumg_ref[sl, :]
        cum_t = cumt_ref[ci]
        dt_t = dtt_ref[ci]
        w_t = wt_ref[ci]
        et_b = etb_ref[ci]
        ys = []
        for p in range(SSD_HPG // 2):
            ms, ws, cols = [], [], []
            for j in (2 * p, 2 * p + 1):
                col = jnp.broadcast_to(cumg[:, j:j + 1], (L, L))
                seg = jnp.where(causal, col - cum_t[j:j + 1, :], NEG_BIG)
                ms.append(cbm * jnp.exp(seg) * dt_t[j:j + 1, :])
                ws.append(b_t * w_t[j:j + 1, :])
                cols.append(col)
            lhs = jnp.concatenate([jnp.concatenate(ms, axis=1), jnp.concatenate(ws, axis=1)],
                                  axis=0).astype(BF16)
            xp = x[:, p * P2:(p + 1) * P2]
            x2 = jnp.concatenate([jnp.where(lo, xp, 0.0), jnp.where(lo, 0.0, xp)],
                                 axis=0).astype(BF16)
            res = jnp.dot(lhs, x2, preferred_element_type=F32)
            ecum = jnp.exp(jnp.where(lo, cols[0], cols[1]))
            ys.append(res[:L] + yoff[:, p * P2:(p + 1) * P2] * ecum)
            et = jnp.where(lo_row, et_b[2 * p:2 * p + 1, :], et_b[2 * p + 1:2 * p + 2, :])
            st[:, p * P2:(p + 1) * P2] = et * s_old[:, p * P2:(p + 1) * P2] + res[L:]
        y = jnp.concatenate(ys, axis=1) + x * d_ref[...]
        y = y * _silu(z_ref[sl, :])
        y = y * lax.rsqrt(jnp.mean(y * y, axis=-1, keepdims=True) + EPS) * nw_ref[...]
        o_ref[sl, :] = y.astype(o_ref.dtype)
        return carry

    lax.fori_loop(0, tt // L, chunk, 0)


def ssd_mixer(proj, prep, conv_w, conv_b, d_skip, norm_w, *, batch, seq, tt=512):
    T = proj.shape[0]
    cumg, cumt, dtt, wt, etb = prep
    kw = conv_w.shape[0]
    G = SSD_GROUPS
    gw = SSD_HPG * SSD_HEADDIM
    inner = G * gw
    N = SSD_STATE
    L = SSD_CHUNK
    tt = min(tt, seq)
    nt = seq // tt
    ncb = tt // L
    row = lambda b, g, t: b * nt + t
    xo, bo, co = inner // gw, 2 * inner // N, (2 * inner + G * N) // N
    d_exp = jnp.repeat(d_skip, SSD_HEADDIM).reshape(1, inner)
    cbias = conv_b.reshape(1, -1)
    sq = lambda: pl.BlockSpec((ncb, SSD_HPG, L), lambda b, g, t: (row(b, g, t), g, 0))
    return pl.pallas_call(
        functools.partial(_ssd_kernel, tt=tt, kw=kw),
        grid=(batch, G, nt),
        in_specs=[pl.BlockSpec((tt, gw), lambda b, g, t: (row(b, g, t), g)),
                  pl.BlockSpec((tt, gw), lambda b, g, t: (row(b, g, t), xo + g)),
                  pl.BlockSpec((tt, N), lambda b, g, t: (row(b, g, t), bo + g)),
                  pl.BlockSpec((tt, N), lambda b, g, t: (row(b, g, t), co + g)),
                  pl.BlockSpec((kw, gw), lambda b, g, t: (0, g)),
                  pl.BlockSpec((kw, N), lambda b, g, t: (0, inner // N + g)),
                  pl.BlockSpec((kw, N), lambda b, g, t: (0, inner // N + G + g)),
                  pl.BlockSpec((1, gw), lambda b, g, t: (0, g)),
                  pl.BlockSpec((1, N), lambda b, g, t: (0, inner // N + g)),
                  pl.BlockSpec((1, N), lambda b, g, t: (0, inner // N + G + g)),
                  pl.BlockSpec((tt, V7X_LANES), lambda b, g, t: (row(b, g, t), g)),
                  sq(), sq(), sq(), sq(),
                  pl.BlockSpec((1, gw), lambda b, g, t: (0, g)),
                  pl.BlockSpec((1, gw), lambda b, g, t: (0, g))],
        out_specs=pl.BlockSpec((tt, gw), lambda b, g, t: (row(b, g, t), g)),
        out_shape=jax.ShapeDtypeStruct((T, inner), BF16),
        scratch_shapes=[pltpu.VMEM((tt + 8, gw), F32), pltpu.VMEM((tt + 8, N), F32),
                        pltpu.VMEM((tt + 8, N), F32), pltpu.VMEM((tt, gw), F32),
                        pltpu.VMEM((tt, N), F32), pltpu.VMEM((tt, N), F32),
                        pltpu.VMEM((N, gw), F32)],
        compiler_params=_cparams(("parallel", "parallel", "arbitrary")),
        name="ssd_mixer",
    )(proj, proj, proj, proj, conv_w, conv_w, conv_w, cbias, cbias, cbias,
      cumg, cumt, dtt, wt, etb, d_exp, norm_w.reshape(1, inner))


def _rmsnorm_kernel(x_ref, w_ref, o_ref):
    x = x_ref[...]
    ms = jnp.mean(x * x, axis=-1, keepdims=True)
    o_ref[...] = x * lax.rsqrt(ms + EPS) * w_ref[...]


def rmsnorm(x, w, *, tm=256):
    T, D = x.shape
    return pl.pallas_call(
        _rmsnorm_kernel,
        grid=(T // tm,),
        in_specs=[pl.BlockSpec((tm, D), lambda i: (i, 0)), pl.BlockSpec((1, D), lambda i: (0, 0))],
        out_specs=pl.BlockSpec((tm, D), lambda i: (i, 0)),
        out_shape=jax.ShapeDtypeStruct((T, D), F32),
        compiler_params=_cparams(("parallel",)),
        name="final_rmsnorm",
    )(x, w.reshape(1, D))


def even_layer(x, nw, w_in, lru_cw, lru_cb, w_r, b_r, w_i, b_i, lam, lower_bounds, hg_nw, w_out,
               *, layer, batch, seq):
    lru_w = lru_cw.shape[1]
    proj = norm_matmul(x, nw, w_in.astype(BF16), tm=1024, tn=1024)
    ya = lru_mixer(proj, lru_cw, lru_cb, w_r, b_r, w_i, b_i, lam, batch=batch, seq=seq)
    yb = hgrn2_mixer(proj, lower_bounds, hg_nw, layer=layer, batch=batch, seq=seq, col0=2 * lru_w)
    return matmul_residual([ya, yb], w_out.astype(BF16), x, tm=512, tn=w_out.shape[1])


def odd_layer(x, nw, w_in, conv_w, conv_b, dt_bias, a_log, d_skip, norm_w, w_out, *, batch, seq):
    D, n_in = w_in.shape
    nh = dt_bias.shape[0]
    dt_col = n_in - nh
    tn = 1152
    n_pad = -(-n_in // tn) * tn
    w = jnp.pad(w_in, ((0, 0), (0, n_pad - n_in))).astype(BF16)
    proj = norm_matmul(x, nw, w, tm=1024, tn=tn)
    prep = ssd_prep(proj, dt_bias, a_log, dt_col=dt_col)
    y = ssd_mixer(proj, prep, conv_w, conv_b, d_skip, norm_w, batch=batch, seq=seq)
    return matmul_residual([y], w_out.astype(BF16), x, tm=512, tn=1024)


def ffn_layer(x, nw, w_up, conv_w, conv_b, w_down, *, seq):
    act = ffn_up(x, nw, w_up.astype(BF16), conv_w, conv_b, seq=seq)
    return matmul_residual([act], w_down.astype(BF16), x, tm=512, tn=1024)


def kernel(x, norm_mix_w, norm_ffn_w, norm_final_w, ev_w_in, lru_conv_w, lru_conv_b, lru_w_r, lru_b_r,
           lru_w_i, lru_b_i, lru_lambda, hg_lower_bounds, hg_norm_w, ev_w_out, ssd_w_in, ssd_conv_w,
           ssd_conv_b, ssd_dt_bias, ssd_a_log, ssd_d, ssd_norm_w, ssd_w_out, ffn_w_up, ffn_conv_w,
           ffn_conv_b, ffn_w_down):
    batch, seq, d = x.shape
    depth = norm_mix_w.shape[0]
    h = x.reshape(batch * seq, d)
    for l in range(depth):
        if l % 2 == 0:
            e = l // 2
            h = even_layer(h, norm_mix_w[l], ev_w_in[e], lru_conv_w[e], lru_conv_b[e], lru_w_r[e],
                           lru_b_r[e], lru_w_i[e], lru_b_i[e], lru_lambda[e], hg_lower_bounds,
                           hg_norm_w[e], ev_w_out[e], layer=e, batch=batch, seq=seq)
        else:
            o = l // 2
            h = odd_layer(h, norm_mix_w[l], ssd_w_in[o], ssd_conv_w[o], ssd_conv_b[o], ssd_dt_bias[o],
                          ssd_a_log[o], ssd_d[o], ssd_norm_w[o], ssd_w_out[o], batch=batch, seq=seq)
        h = ffn_layer(h, norm_ffn_w[l], ffn_w_up[l], ffn_conv_w[l], ffn_conv_b[l], ffn_w_down[l], seq=seq)
    return rmsnorm(h, norm_final_w).reshape(batch, seq, d)
```

```python
import functools

import jax
import jax.numpy as jnp
from jax import lax
from jax.experimental import pallas as pl
from jax.experimental.pallas import tpu as pltpu

F32 = jnp.float32
BF16 = jnp.bfloat16
EPS = 1e-6

V7X_LANES = 128
V7X_SUBLANES = 8
V7X_MXU_DIM = 256
V7X_VMEM_BYTES = 64 * 1024 * 1024
VMEM_LIMIT = V7X_VMEM_BYTES - 8 * 1024 * 1024

LRU_BLOCKS = 16
LRU_C = 8.0
HG_HEADS = 8
HG_DK = 128
HG_CHUNK = 64
SSD_HEADDIM = 64
SSD_GROUPS = 8
SSD_HPG = 8
SSD_STATE = 128
SSD_CHUNK = 128
NEG_BIG = -1e30

ROW_TILE = 1024
NORM_ROWS = 128
IN_PROJ_TN = 1024
FFN_TC = 512
CONV_SUB = V7X_MXU_DIM
CONV_ROWS = 64
OUT_TM = 512
OUT_TN = 1024


def _cparams(sem):
    return pltpu.CompilerParams(dimension_semantics=sem, vmem_limit_bytes=VMEM_LIMIT)


def _sigmoid(x):
    return 1.0 / (1.0 + jnp.exp(-x))


def _silu(x):
    return x * _sigmoid(x)


def _gelu_tanh(x):
    return 0.5 * x * (1.0 + jnp.tanh(0.7978845608028654 * (x + 0.044715 * (x * x * x))))


def _split3(x):
    hi = x.astype(BF16)
    r1 = x - hi.astype(F32)
    mid = r1.astype(BF16)
    lo = (r1 - mid.astype(F32)).astype(BF16)
    return jnp.concatenate([hi, mid, lo], axis=1)


def _cumsum_rows(x):
    n, w = x.shape
    r = lax.broadcasted_iota(jnp.int32, (n, n), 0)
    c = lax.broadcasted_iota(jnp.int32, (n, n), 1)
    tril = jnp.where(r >= c, 1.0, 0.0).astype(BF16)
    p = jnp.dot(tril, _split3(x), preferred_element_type=F32)
    return p[:, :w] + p[:, w:2 * w] + p[:, 2 * w:]


def _normalise_rows(x_ref, nw_ref, h_ref, rows):
    def body(r, c):
        sl = pl.ds(pl.multiple_of(r * rows, rows), rows)
        x = x_ref[sl, :]
        ms = jnp.mean(x * x, axis=-1, keepdims=True)
        h_ref[sl, :] = (x * lax.rsqrt(ms + EPS) * nw_ref[...]).astype(BF16)
        return c
    lax.fori_loop(0, x_ref.shape[0] // rows, body, 0)


def _cast_kernel(x_ref, o_ref):
    o_ref[...] = x_ref[...].astype(o_ref.dtype)


def cast_bf16(w):
    L, K, N = w.shape
    if N % V7X_LANES == 0:
        tk = 512
        tn = max(t for t in range(V7X_LANES, 2816 + 1, V7X_LANES) if N % t == 0)
    else:
        tk, tn = 128, N
    return pl.pallas_call(
        _cast_kernel,
        grid=(L, K // tk, N // tn),
        in_specs=[pl.BlockSpec((1, tk, tn), lambda l, i, j: (l, i, j))],
        out_specs=pl.BlockSpec((1, tk, tn), lambda l, i, j: (l, i, j)),
        out_shape=jax.ShapeDtypeStruct(w.shape, BF16),
        compiler_params=_cparams(("parallel", "parallel", "parallel")),
        name="cast_bf16",
    )(w)


def _norm_mm_kernel(x_ref, nw_ref, w_ref, o_ref, h_ref, *, rows):
    @pl.when(pl.program_id(1) == 0)
    def _():
        _normalise_rows(x_ref, nw_ref, h_ref, rows)

    o_ref[...] = jnp.dot(h_ref[...], w_ref[...], preferred_element_type=F32).astype(o_ref.dtype)


def norm_matmul(x, nw, w, layer, *, tm=ROW_TILE, tn=IN_PROJ_TN):
    T, D = x.shape
    N = w.shape[2]
    tm = min(tm, T)
    return pl.pallas_call(
        functools.partial(_norm_mm_kernel, rows=NORM_ROWS),
        grid=(T // tm, N // tn),
        in_specs=[pl.BlockSpec((tm, D), lambda i, j: (i, 0)),
                  pl.BlockSpec((1, D), lambda i, j: (0, 0)),
                  pl.BlockSpec((None, D, tn), lambda i, j: (layer, 0, j))],
        out_specs=pl.BlockSpec((tm, tn), lambda i, j: (i, j)),
        out_shape=jax.ShapeDtypeStruct((T, N), F32),
        scratch_shapes=[pltpu.VMEM((tm, D), BF16)],
        compiler_params=_cparams(("parallel", "arbitrary")),
        name="norm_matmul",
    )(x, nw.reshape(1, D), w)


def _mm_res_kernel(*refs, nl):
    lhs = refs[:nl]
    w_ref, r_ref, o_ref = refs[nl], refs[nl + 1], refs[nl + 2]
    acc = r_ref[...]
    k0 = 0
    for l in lhs:
        k = l.shape[1]
        acc = acc + jnp.dot(l[...], w_ref[k0:k0 + k, :], preferred_element_type=F32)
        k0 += k
    o_ref[...] = acc


def matmul_residual(lhs_list, w, layer, res, *, tm=OUT_TM, tn=OUT_TN):
    T, N = res.shape
    K = w.shape[1]
    tm = min(tm, T)
    nl = len(lhs_list)
    in_specs = [pl.BlockSpec((tm, l.shape[1]), lambda j, i: (i, 0)) for l in lhs_list]
    in_specs += [pl.BlockSpec((None, K, tn), lambda j, i: (layer, 0, j)),
                 pl.BlockSpec((tm, tn), lambda j, i: (i, j))]
    return pl.pallas_call(
        functools.partial(_mm_res_kernel, nl=nl),
        grid=(N // tn, T // tm),
        in_specs=in_specs,
        out_specs=pl.BlockSpec((tm, tn), lambda j, i: (i, j)),
        out_shape=jax.ShapeDtypeStruct((T, N), F32),
        compiler_params=_cparams(("parallel", "parallel")),
        name="matmul_residual",
    )(*lhs_list, w, res)


def _proj_conv_kernel(*refs, nb, tm, nj, nsteps, seq, kw, rows, sub, erows, aux):
    it = iter(refs)
    x_ref, nw_ref = next(it), next(it)
    w_refs = [next(it) for _ in range(nb)]
    cw_refs = [next(it) for _ in range(nb)]
    cb_refs = [next(it) for _ in range(nb)]
    aux_w = next(it) if aux else None
    o_ref = next(it)
    aux_o = next(it) if aux else None
    h_ref, pad_a, pad_b, halo = next(it), next(it), next(it), next(it)

    n = pl.program_id(0)
    j = n % nj
    nsub = o_ref.shape[1] // sub

    @pl.when(n == 0)
    def _():
        pad_b[...] = jnp.zeros_like(pad_b)

    @pl.when((j == 0) & (n < nsteps))
    def _():
        _normalise_rows(x_ref, nw_ref, h_ref, rows)
        if aux:
            aux_o[...] = jnp.dot(h_ref[...], aux_w[...], preferred_element_type=F32)

    first = ((n // nj) * tm) % seq == 0

    def epilogue(prev, s, r0):
        cs = slice(s * sub, (s + 1) * sub)
        ys = []
        for b in range(nb):
            blk = prev[b, s, r0:r0 + erows + 8, :]
            y = cb_refs[b][:, cs] + cw_refs[b][kw - 1:kw, cs] * blk[8:, :]
            for k in range(kw - 1):
                y = y + cw_refs[b][k:k + 1, cs] * pltpu.roll(blk, kw - 1 - k, axis=0)[8:, :]
            ys.append(y)
        act = _silu(ys[0])
        if nb == 2:
            act = act * ys[1]
        o_ref[r0:r0 + erows, cs] = act.astype(o_ref.dtype)

    def product(cur, s, b):
        cs = slice(s * sub, (s + 1) * sub)
        cur[b, s, 0:8, :] = jnp.where(first, 0.0, halo[j, b, :, cs])
        cur[b, s, 8:8 + tm, :] = jnp.dot(h_ref[...], w_refs[b][:, cs], preferred_element_type=F32)
        halo[j, b, :, cs] = cur[b, s, tm:tm + 8, :]

    def step(cur, prev):
        dots = [(s, b) for s in range(nsub) for b in range(nb)]
        blocks = [(s, r0) for s in range(nsub) for r0 in range(0, tm, erows)]
        per = -(-len(blocks) // len(dots))
        for d, (s, b) in enumerate(dots):
            for (es, r0) in blocks[d * per:(d + 1) * per]:
                epilogue(prev, es, r0)
            product(cur, s, b)

    @pl.when(n % 2 == 0)
    def _():
        step(pad_a, pad_b)

    @pl.when(n % 2 == 1)
    def _():
        step(pad_b, pad_a)


def proj_conv(x, nw, w, layer, cw, cb, conv_layer, *, nb, seq, tc, aux_w=None,
              tm=ROW_TILE, sub=CONV_SUB):
    T, D = x.shape
    kw = cw.shape[1]
    nj = cw.shape[2] // (nb * tc)
    tm = min(tm, seq)
    nrow = T // tm
    nsteps = nrow * nj
    aux = aux_w is not None

    def prev(n):
        return jnp.maximum(n - 1, 0)

    xrow = lambda n: (jnp.minimum(n // nj, nrow - 1), 0)
    in_specs = [pl.BlockSpec((tm, D), xrow), pl.BlockSpec((1, D), lambda n: (0, 0))]
    in_specs += [pl.BlockSpec((None, D, tc), functools.partial(lambda n, b: (layer, 0, n % nj + b * nj), b=b))
                 for b in range(nb)]
    in_specs += [pl.BlockSpec((None, kw, tc),
                              functools.partial(lambda n, b: (conv_layer, 0, prev(n) % nj + b * nj), b=b))
                 for b in range(nb)]
    in_specs += [pl.BlockSpec((None, 1, tc),
                              functools.partial(lambda n, b: (conv_layer, 0, prev(n) % nj + b * nj), b=b))
                 for b in range(nb)]
    args = [x, nw.reshape(1, D)] + [w] * nb + [cw] * nb + [cb] * nb
    out_specs = [pl.BlockSpec((tm, tc), lambda n: (prev(n) // nj, prev(n) % nj))]
    out_shape = [jax.ShapeDtypeStruct((T, nj * tc), BF16)]
    if aux:
        in_specs.append(pl.BlockSpec(aux_w.shape, lambda n: (0, 0)))
        args.append(aux_w)
        out_specs.append(pl.BlockSpec((tm, aux_w.shape[1]), xrow))
        out_shape.append(jax.ShapeDtypeStruct((T, aux_w.shape[1]), F32))
    pad = pltpu.VMEM((nb, tc // sub, tm + 8, sub), F32)
    outs = pl.pallas_call(
        functools.partial(_proj_conv_kernel, nb=nb, tm=tm, nj=nj, nsteps=nsteps, seq=seq, kw=kw,
                          rows=NORM_ROWS, sub=sub, erows=min(CONV_ROWS, tm), aux=aux),
        grid=(nsteps + 1,),
        in_specs=in_specs,
        out_specs=out_specs,
        out_shape=out_shape,
        scratch_shapes=[pltpu.VMEM((tm, D), BF16), pad, pad, pltpu.VMEM((nj, nb, 8, tc), F32)],
        compiler_params=_cparams(("arbitrary",)),
        name="proj_conv",
    )(*args)
    return outs if aux else outs[0]


def _lru_kernel(xa_ref, ga_ref, cw_ref, cb_ref, wr_ref, br_ref, wi_ref, bi_ref, lam_ref, o_ref,
                xpad, a_s, u_s, hc, *, tt, kw):
    W = xa_ref.shape[1]

    @pl.when(pl.program_id(1) == 0)
    def _():
        xpad[0:8, :] = jnp.zeros((8, W), F32)
        hc[...] = jnp.zeros_like(hc)

    xpad[8:8 + tt, :] = xa_ref[...]
    full = xpad[...]
    xc = cb_ref[...] + cw_ref[kw - 1:kw, :] * full[8:, :]
    for k in range(kw - 1):
        xc = xc + cw_ref[k:k + 1, :] * pltpu.roll(full, kw - 1 - k, axis=0)[8:, :]
    xpad[0:8, :] = full[tt:tt + 8, :]

    xb = xc.astype(BF16)
    nt = W // V7X_MXU_DIM

    def gate(w_ref, b_ref):
        parts = [jnp.dot(xb[:, j * V7X_MXU_DIM:(j + 1) * V7X_MXU_DIM], w_ref[j],
                         preferred_element_type=F32) for j in range(nt)]
        return _sigmoid(jnp.concatenate(parts, axis=1) + b_ref[...])

    r = gate(wr_ref, br_ref)
    gi = gate(wi_ref, bi_ref)
    nl = -lam_ref[...]
    sp = jnp.maximum(nl, 0.0) + jnp.log1p(jnp.exp(-jnp.abs(nl)))
    log_a = (-LRU_C) * r * sp
    a_s[...] = jnp.exp(log_a)
    th = jnp.tanh(log_a)
    u_s[...] = jnp.sqrt(-2.0 * th / (1.0 - th)) * (gi * xc)

    rows = lax.broadcasted_iota(jnp.int32, (8, W), 0)

    def body(i, h):
        sl = pl.ds(pl.multiple_of(i * 8, 8), 8)
        a = a_s[sl, :]
        u = u_s[sl, :]
        for d in (1, 2, 4):
            m = rows >= d
            a_sh = pltpu.roll(a, d, axis=0)
            u_sh = pltpu.roll(u, d, axis=0)
            u = jnp.where(m, a * u_sh + u, u)
            a = jnp.where(m, a * a_sh, a)
        hh = a * h + u
        a_s[sl, :] = hh
        return hh[7:8, :]

    hc[0:1, :] = lax.fori_loop(0, tt // 8, body, hc[0:1, :])
    o_ref[...] = (a_s[...] * _gelu_tanh(ga_ref[...])).astype(o_ref.dtype)


def _blockdiag_tiles(w):
    nb, k, _ = w.shape
    per = V7X_MXU_DIM // k
    w4 = w.reshape(nb // per, per, k, k)
    eye = jnp.eye(per, dtype=w.dtype)
    t = jnp.einsum('tbij,bc->tbicj', w4, eye)
    return t.reshape(nb // per, V7X_MXU_DIM, V7X_MXU_DIM)


def lru_mixer(proj, cw, cb, w_r, b_r, w_i, b_i, lam, *, batch, seq, tt=256):
    T = proj.shape[0]
    kw, W = cw.shape
    tt = min(tt, seq)
    nt = seq // tt
    wr = _blockdiag_tiles(w_r).astype(BF16)
    wi = _blockdiag_tiles(w_i).astype(BF16)
    vec = lambda b, t: (0, 0)
    wspec = pl.BlockSpec(wr.shape, lambda b, t: (0, 0, 0))
    return pl.pallas_call(
        functools.partial(_lru_kernel, tt=tt, kw=kw),
        grid=(batch, nt),
        in_specs=[pl.BlockSpec((tt, W), lambda b, t: (b * nt + t, 0)),
                  pl.BlockSpec((tt, W), lambda b, t: (b * nt + t, 1)),
                  pl.BlockSpec((kw, W), vec), pl.BlockSpec((1, W), vec),
                  wspec, pl.BlockSpec((1, W), vec),
                  wspec, pl.BlockSpec((1, W), vec),
                  pl.BlockSpec((1, W), vec)],
        out_specs=pl.BlockSpec((tt, W), lambda b, t: (b * nt + t, 0)),
        out_shape=jax.ShapeDtypeStruct((T, W), BF16),
        scratch_shapes=[pltpu.VMEM((tt + 8, W), F32), pltpu.VMEM((tt, W), F32),
                        pltpu.VMEM((tt, W), F32), pltpu.VMEM((8, W), F32)],
        compiler_params=_cparams(("parallel", "arbitrary")),
        name="lru_mixer",
    )(proj, proj, cw, cb.reshape(1, W), wr, b_r.reshape(1, W), wi, b_i.reshape(1, W),
      lam.reshape(1, W))


def _hgrn2_kernel(q_ref, f_ref, v_ref, g_ref, lb_ref, nw_ref, o_ref, st, *, tt, layer, hb):
    C = HG_CHUNK

    @pl.when(pl.program_id(2) == 0)
    def _():
        st[...] = jnp.zeros_like(st)

    raw = lb_ref[...]
    e = jnp.exp(raw - jnp.max(raw, axis=0, keepdims=True))
    p = e / jnp.sum(e, axis=0, keepdims=True)
    lb_all = -p[0:1, :]
    for i in range(layer + 1):
        lb_all = lb_all + p[i:i + 1, :]

    r = lax.broadcasted_iota(jnp.int32, (C, C), 0)
    c = lax.broadcasted_iota(jnp.int32, (C, C), 1)
    causal = r >= c
    scale = HG_DK ** -0.5
    nw = nw_ref[...]

    nt_dims = (((1,), (1,)), ((), ()))
    tn_dims = (((0,), (0,)), ((), ()))
    heads = range(hb)
    cols = [slice(h * HG_DK, (h + 1) * HG_DK) for h in heads]

    def chunk(ci, carry):
        sl = pl.ds(pl.multiple_of(ci * C, C), C)
        ks, cums = [], []
        for h in heads:
            lb = lb_all[:, cols[h]]
            f = lb + (1.0 - lb) * _sigmoid(f_ref[sl, cols[h]])
            ks.append(1.0 - f)
            cums.append(_cumsum_rows(jnp.log(f)))
        qss, scs = [], []
        for h in heads:
            cum, k = cums[h], ks[h]
            mid = cum[C // 2 - 1:C // 2, :]
            qs = _silu(q_ref[sl, cols[h]]) * scale
            qd = (qs * jnp.exp(cum - mid)).astype(BF16)
            kd = (k * jnp.exp(mid - cum)).astype(BF16)
            qss.append(qs)
            scs.append(lax.dot_general(qd, kd, nt_dims, preferred_element_type=F32))
        outs = []
        for h in heads:
            cum, k = cums[h], ks[h]
            tot = cum[C - 1:C, :]
            vb = v_ref[sl, cols[h]].astype(BF16)
            sc = jnp.where(causal, scs[h], 0.0).astype(BF16)
            s_old = st[h]
            o = jnp.dot(sc, vb, preferred_element_type=F32)
            o = o + lax.dot_general((qss[h] * jnp.exp(cum)).astype(BF16), s_old.astype(BF16),
                                    nt_dims, preferred_element_type=F32)
            kdec = (k * jnp.exp(tot - cum)).astype(BF16)
            upd = lax.dot_general(vb, kdec, tn_dims, preferred_element_type=F32)
            st[h] = jnp.exp(tot) * s_old + upd
            outs.append(o)
        for h in heads:
            o = outs[h]
            o = o * lax.rsqrt(jnp.mean(o * o, axis=-1, keepdims=True) + EPS) * nw
            o_ref[sl, cols[h]] = (o * _silu(g_ref[sl, cols[h]])).astype(o_ref.dtype)
        return carry

    lax.fori_loop(0, tt // C, chunk, 0)


def hgrn2_mixer(proj, lower_bounds, norm_w, *, layer, batch, seq, col0, tt=256, hb=8):
    T = proj.shape[0]
    tt = min(tt, seq)
    nt = seq // tt
    H = HG_HEADS
    bw = hb * HG_DK
    c0 = col0 // bw
    nhb = H // hb

    def col(k):
        return pl.BlockSpec((tt, bw), lambda b, h, t: (b * nt + t, c0 + k * nhb + h))

    ne = lower_bounds.shape[0]
    return pl.pallas_call(
        functools.partial(_hgrn2_kernel, tt=tt, layer=layer, hb=hb),
        grid=(batch, nhb, nt),
        in_specs=[col(0), col(1), col(2), col(3),
                  pl.BlockSpec((ne, bw), lambda b, h, t: (0, h)),
                  pl.BlockSpec((1, HG_DK), lambda b, h, t: (0, 0))],
        out_specs=pl.BlockSpec((tt, bw), lambda b, h, t: (b * nt + t, h)),
        out_shape=jax.ShapeDtypeStruct((T, H * HG_DK), BF16),
        scratch_shapes=[pltpu.VMEM((hb, HG_DK, HG_DK), F32)],
        compiler_params=_cparams(("parallel", "parallel", "arbitrary")),
        name="hgrn2_mixer",
    )(proj, proj, proj, proj, lower_bounds, norm_w.reshape(1, HG_DK))


def _ssd_prep_kernel(dt_ref, bias_ref, alog_ref, cumg_ref, cumt_ref, dtt_ref, wt_ref, etb_ref):
    L = SSD_CHUNK
    x = dt_ref[...] + bias_ref[...]
    dt = jnp.maximum(x, 0.0) + jnp.log1p(jnp.exp(-jnp.abs(x)))
    a_neg = -jnp.exp(alog_ref[...])
    cum = _cumsum_rows(dt * a_neg)
    for g in range(SSD_GROUPS):
        sh = (V7X_LANES - SSD_HPG * g) % V7X_LANES
        cumg_ref[:, g * V7X_LANES:(g + 1) * V7X_LANES] = cum if sh == 0 else pltpu.roll(cum, sh, axis=1)
    cum_t = cum.T
    dt_t = dt.T
    tot = jnp.broadcast_to(cum_t[:, L - 1:L], cum_t.shape)
    cumt_ref[0] = cum_t
    dtt_ref[0] = dt_t
    wt_ref[0] = dt_t * jnp.exp(tot - cum_t)
    etb_ref[0] = jnp.exp(tot)


def ssd_prep(dt_raw, dt_bias, a_log):
    T = dt_raw.shape[0]
    L = SSD_CHUNK
    nc = T // L
    nh = dt_bias.shape[0]
    pad = V7X_LANES - nh
    bias = jnp.pad(dt_bias, (0, pad)).reshape(1, V7X_LANES)
    alog = jnp.pad(a_log, (0, pad)).reshape(1, V7X_LANES)
    sq = jax.ShapeDtypeStruct((nc, V7X_LANES, L), F32)
    sqspec = pl.BlockSpec((1, V7X_LANES, L), lambda i: (i, 0, 0))
    return pl.pallas_call(
        _ssd_prep_kernel,
        grid=(nc,),
        in_specs=[pl.BlockSpec((L, V7X_LANES), lambda i: (i, 0)),
                  pl.BlockSpec((1, V7X_LANES), lambda i: (0, 0)),
                  pl.BlockSpec((1, V7X_LANES), lambda i: (0, 0))],
        out_specs=[pl.BlockSpec((L, SSD_GROUPS * V7X_LANES), lambda i: (i, 0)),
                   sqspec, sqspec, sqspec, sqspec],
        out_shape=[jax.ShapeDtypeStruct((T, SSD_GROUPS * V7X_LANES), F32), sq, sq, sq, sq],
        compiler_params=_cparams(("parallel",)),
        name="ssd_prep",
    )(dt_raw, bias, alog)


def _ssd_kernel(z_ref, x_ref, b_ref, c_ref, cumg_ref, cumt_ref, dtt_ref, wt_ref, etb_ref,
                d_ref, nw_ref, o_ref, st, *, tt):
    L = SSD_CHUNK
    P2 = 2 * SSD_HEADDIM

    @pl.when(pl.program_id(2) == 0)
    def _():
        st[...] = jnp.zeros_like(st)

    r = lax.broadcasted_iota(jnp.int32, (L, L), 0)
    c = lax.broadcasted_iota(jnp.int32, (L, L), 1)
    causal = r >= c
    lo = lax.broadcasted_iota(jnp.int32, (L, P2), 1) < SSD_HEADDIM
    lo_row = lo[0:1, :]

    def chunk(ci, carry):
        sl = pl.ds(pl.multiple_of(ci * L, L), L)
        x = x_ref[sl, :].astype(F32)
        b16 = b_ref[sl, :]
        c16 = c_ref[sl, :]
        b_t = b16.astype(F32).T
        cbm = lax.dot_general(c16, b16, (((1,), (1,)), ((), ())),
                              preferred_element_type=F32)
        s_old = st[...]
        yoff = jnp.dot(c16, s_old.astype(BF16), preferred_element_type=F32)
        cumg = cumg_ref[sl, :]
        cum_t = cumt_ref[ci]
        dt_t = dtt_ref[ci]
        w_t = wt_ref[ci]
        et_b = etb_ref[ci]
        ys = []
        for p in range(SSD_HPG // 2):
            ms, ws, cols = [], [], []
            for j in (2 * p, 2 * p + 1):
                col = jnp.broadcast_to(cumg[:, j:j + 1], (L, L))
                seg = jnp.where(causal, col - cum_t[j:j + 1, :], NEG_BIG)
                ms.append(cbm * jnp.exp(seg) * dt_t[j:j + 1, :])
                ws.append(b_t * w_t[j:j + 1, :])
                cols.append(col)
            lhs = jnp.concatenate([jnp.concatenate(ms, axis=1), jnp.concatenate(ws, axis=1)],
                                  axis=0).astype(BF16)
            xp = x[:, p * P2:(p + 1) * P2]
            x2 = jnp.concatenate([jnp.where(lo, xp, 0.0), jnp.where(lo, 0.0, xp)],
                                 axis=0).astype(BF16)
            res = jnp.dot(lhs, x2, preferred_element_type=F32)
            ecum = jnp.exp(jnp.where(lo, cols[0], cols[1]))
            ys.append(res[:L] + yoff[:, p * P2:(p + 1) * P2] * ecum)
            et = jnp.where(lo_row, et_b[2 * p:2 * p + 1, :], et_b[2 * p + 1:2 * p + 2, :])
            st[:, p * P2:(p + 1) * P2] = et * s_old[:, p * P2:(p + 1) * P2] + res[L:]
        y = jnp.concatenate(ys, axis=1) + x * d_ref[...]
        y = y * z_ref[sl, :].astype(F32)
        y = y * lax.rsqrt(jnp.mean(y * y, axis=-1, keepdims=True) + EPS) * nw_ref[...]
        o_ref[sl, :] = y.astype(o_ref.dtype)
        return carry

    lax.fori_loop(0, tt // L, chunk, 0)


def ssd_mixer(act, prep, d_skip, norm_w, *, batch, seq, tt=512):
    T = act.shape[0]
    cumg, cumt, dtt, wt, etb = prep
    G = SSD_GROUPS
    gw = SSD_HPG * SSD_HEADDIM
    inner = G * gw
    N = SSD_STATE
    L = SSD_CHUNK
    tt = min(tt, seq)
    nt = seq // tt
    ncb = tt // L
    row = lambda b, g, t: b * nt + t
    xo, bo, co = inner // gw, 2 * inner // N, (2 * inner + G * N) // N
    d_exp = jnp.repeat(d_skip, SSD_HEADDIM).reshape(1, inner)
    sq = lambda: pl.BlockSpec((ncb, SSD_HPG, L), lambda b, g, t: (row(b, g, t), g, 0))
    return pl.pallas_call(
        functools.partial(_ssd_kernel, tt=tt),
        grid=(batch, G, nt),
        in_specs=[pl.BlockSpec((tt, gw), lambda b, g, t: (row(b, g, t), g)),
                  pl.BlockSpec((tt, gw), lambda b, g, t: (row(b, g, t), xo + g)),
                  pl.BlockSpec((tt, N), lambda b, g, t: (row(b, g, t), bo + g)),
                  pl.BlockSpec((tt, N), lambda b, g, t: (row(b, g, t), co + g)),
                  pl.BlockSpec((tt, V7X_LANES), lambda b, g, t: (row(b, g, t), g)),
                  sq(), sq(), sq(), sq(),
                  pl.BlockSpec((1, gw), lambda b, g, t: (0, g)),
                  pl.BlockSpec((1, gw), lambda b, g, t: (0, g))],
        out_specs=pl.BlockSpec((tt, gw), lambda b, g, t: (row(b, g, t), g)),
        out_shape=jax.ShapeDtypeStruct((T, inner), BF16),
        scratch_shapes=[pltpu.VMEM((N, gw), F32)],
        compiler_params=_cparams(("parallel", "parallel", "arbitrary")),
        name="ssd_mixer",
    )(act, act, act, act, cumg, cumt, dtt, wt, etb, d_exp, norm_w.reshape(1, inner))


def _rmsnorm_kernel(x_ref, w_ref, o_ref):
    x = x_ref[...]
    ms = jnp.mean(x * x, axis=-1, keepdims=True)
    o_ref[...] = x * lax.rsqrt(ms + EPS) * w_ref[...]


def rmsnorm(x, w, *, tm=256):
    T, D = x.shape
    return pl.pallas_call(
        _rmsnorm_kernel,
        grid=(T // tm,),
        in_specs=[pl.BlockSpec((tm, D), lambda i: (i, 0)), pl.BlockSpec((1, D), lambda i: (0, 0))],
        out_specs=pl.BlockSpec((tm, D), lambda i: (i, 0)),
        out_shape=jax.ShapeDtypeStruct((T, D), F32),
        compiler_params=_cparams(("parallel",)),
        name="final_rmsnorm",
    )(x, w.reshape(1, D))


def even_layer(x, nw, w_in, w_out, e, lru_cw, lru_cb, w_r, b_r, w_i, b_i, lam, lower_bounds, hg_nw,
               *, batch, seq):
    lru_w = lru_cw.shape[1]
    proj = norm_matmul(x, nw, w_in, e)
    ya = lru_mixer(proj, lru_cw, lru_cb, w_r, b_r, w_i, b_i, lam, batch=batch, seq=seq)
    yb = hgrn2_mixer(proj, lower_bounds, hg_nw, layer=e, batch=batch, seq=seq, col0=2 * lru_w)
    return matmul_residual([ya, yb], w_out, e, x, tn=w_out.shape[2])


def odd_layer(x, nw, w_in, w_out, o, w_dt, conv_w, conv_b, dt_bias, a_log, d_skip, norm_w,
              *, batch, seq):
    kw, conv_dim = conv_w.shape
    inner = norm_w.shape[0]
    nh = dt_bias.shape[0]
    ident = jnp.zeros((kw, inner), F32).at[kw - 1].set(1.0)
    cw = jnp.concatenate([ident, conv_w], axis=1)[None]
    cb = jnp.concatenate([jnp.zeros((inner,), F32), conv_b]).reshape(1, 1, inner + conv_dim)
    aux_w = jnp.pad(w_dt, ((0, 0), (0, V7X_LANES - nh))).astype(BF16)
    act, dt_raw = proj_conv(x, nw, w_in, o, cw, cb, 0, nb=1, seq=seq, tc=IN_PROJ_TN, aux_w=aux_w)
    prep = ssd_prep(dt_raw, dt_bias, a_log)
    y = ssd_mixer(act, prep, d_skip, norm_w, batch=batch, seq=seq)
    return matmul_residual([y], w_out, o, x)


def ffn_layer(x, nw, w_up, w_down, l, conv_w, conv_b, *, seq):
    act = proj_conv(x, nw, w_up, l, conv_w, conv_b, l, nb=2, seq=seq, tc=FFN_TC)
    return matmul_residual([act], w_down, l, x)


def kernel(x, norm_mix_w, norm_ffn_w, norm_final_w, ev_w_in, lru_conv_w, lru_conv_b, lru_w_r, lru_b_r,
           lru_w_i, lru_b_i, lru_lambda, hg_lower_bounds, hg_norm_w, ev_w_out, ssd_w_in, ssd_conv_w,
           ssd_conv_b, ssd_dt_bias, ssd_a_log, ssd_d, ssd_norm_w, ssd_w_out, ffn_w_up, ffn_conv_w,
           ffn_conv_b, ffn_w_down):
    batch, seq, d = x.shape
    depth = norm_mix_w.shape[0]
    nh = ssd_dt_bias.shape[1]
    ev_in, ev_out = cast_bf16(ev_w_in), cast_bf16(ev_w_out)
    ssd_in, ssd_out = cast_bf16(ssd_w_in), cast_bf16(ssd_w_out)
    up, down = cast_bf16(ffn_w_up), cast_bf16(ffn_w_down)
    ffn_cb = ffn_conv_b.reshape(depth, 1, -1)
    h = x.reshape(batch * seq, d)
    for l in range(depth):
        if l % 2 == 0:
            e = l // 2
            h = even_layer(h, norm_mix_w[l], ev_in, ev_out, e, lru_conv_w[e], lru_conv_b[e], lru_w_r[e],
                           lru_b_r[e], lru_w_i[e], lru_b_i[e], lru_lambda[e], hg_lower_bounds,
                           hg_norm_w[e], batch=batch, seq=seq)
        else:
            o = l // 2
            h = odd_layer(h, norm_mix_w[l], ssd_in, ssd_out, o, ssd_w_in[o][:, -nh:], ssd_conv_w[o],
                          ssd_conv_b[o], ssd_dt_bias[o], ssd_a_log[o], ssd_d[o], ssd_norm_w[o],
                          batch=batch, seq=seq)
        h = ffn_layer(h, norm_ffn_w[l], up, down, l, ffn_conv_w, ffn_cb, seq=seq)
    return rmsnorm(h, norm_final_w).reshape(batch, seq, d)
```

```python
import functools

import jax
import jax.numpy as jnp
from jax import lax
from jax.experimental import pallas as pl
from jax.experimental.pallas import tpu as pltpu

F32 = jnp.float32
BF16 = jnp.bfloat16
EPS = 1e-6

V7X_LANES = 128
V7X_SUBLANES = 8
V7X_MXU_DIM = 256
V7X_VMEM_BYTES = 64 * 1024 * 1024
VMEM_LIMIT = V7X_VMEM_BYTES - 8 * 1024 * 1024

LRU_BLOCKS = 16
LRU_C = 8.0
HG_HEADS = 8
HG_DK = 128
HG_CHUNK = 64
SSD_HEADDIM = 64
SSD_GROUPS = 8
SSD_HPG = 8
SSD_STATE = 128
SSD_CHUNK = 128
SSD_GROUPS_PER_STEP = 2
NEG_BIG = -1e30
LOG2E = 1.4426950408889634

ROW_TILE = 1024
NORM_ROWS = 128
IN_PROJ_TN = 1024
FFN_TC = 512
CONV_SUB = V7X_MXU_DIM
CONV_BLOCK = 1024 * 256
OUT_TM = 512
OUT_TN = 1024


def _cparams(sem):
    return pltpu.CompilerParams(dimension_semantics=sem, vmem_limit_bytes=VMEM_LIMIT)


def _sigmoid(x):
    return 1.0 / (1.0 + jnp.exp(-x))


def _silu(x):
    return x * _sigmoid(x)


def _gelu_tanh(x):
    return 0.5 * x * (1.0 + jnp.tanh(0.7978845608028654 * (x + 0.044715 * (x * x * x))))


def _split3(x):
    hi = x.astype(BF16)
    r1 = x - hi.astype(F32)
    mid = r1.astype(BF16)
    lo = (r1 - mid.astype(F32)).astype(BF16)
    return jnp.concatenate([hi, mid, lo], axis=1)


def _cumsum_rows(x):
    n, w = x.shape
    r = lax.broadcasted_iota(jnp.int32, (n, n), 0)
    c = lax.broadcasted_iota(jnp.int32, (n, n), 1)
    tril = jnp.where(r >= c, 1.0, 0.0).astype(BF16)
    p = jnp.dot(tril, _split3(x), preferred_element_type=F32)
    return p[:, :w] + p[:, w:2 * w] + p[:, 2 * w:]


def _normalise_rows(x_ref, nw_ref, h_ref, rows):
    def body(r, c):
        sl = pl.ds(pl.multiple_of(r * rows, rows), rows)
        x = x_ref[sl, :]
        ms = jnp.mean(x * x, axis=-1, keepdims=True)
        h_ref[sl, :] = (x * lax.rsqrt(ms + EPS) * nw_ref[...]).astype(BF16)
        return c
    lax.fori_loop(0, x_ref.shape[0] // rows, body, 0)


def _cast_kernel(x_ref, o_ref):
    o_ref[...] = x_ref[...].astype(o_ref.dtype)


def cast_bf16(w, ncols=None):
    L, K, _ = w.shape
    N = w.shape[2] if ncols is None else ncols
    tk = 512
    tn = max(t for t in range(V7X_LANES, 2816 + 1, V7X_LANES) if N % t == 0)
    return pl.pallas_call(
        _cast_kernel,
        grid=(L, K // tk, N // tn),
        in_specs=[pl.BlockSpec((1, tk, tn), lambda l, i, j: (l, i, j))],
        out_specs=pl.BlockSpec((1, tk, tn), lambda l, i, j: (l, i, j)),
        out_shape=jax.ShapeDtypeStruct((L, K, N), BF16),
        compiler_params=_cparams(("parallel", "parallel", "parallel")),
        name="cast_bf16",
    )(w)


def _norm_mm_kernel(x_ref, nw_ref, w_ref, o_ref, h_ref, *, rows):
    @pl.when(pl.program_id(1) == 0)
    def _():
        _normalise_rows(x_ref, nw_ref, h_ref, rows)

    o_ref[...] = jnp.dot(h_ref[...], w_ref[...], preferred_element_type=F32).astype(o_ref.dtype)


def norm_matmul(x, nw, w, layer, *, tm=ROW_TILE, tn=IN_PROJ_TN):
    T, D = x.shape
    N = w.shape[2]
    tm = min(tm, T)
    return pl.pallas_call(
        functools.partial(_norm_mm_kernel, rows=NORM_ROWS),
        grid=(T // tm, N // tn),
        in_specs=[pl.BlockSpec((tm, D), lambda i, j: (i, 0)),
                  pl.BlockSpec((1, D), lambda i, j: (0, 0)),
                  pl.BlockSpec((None, D, tn), lambda i, j: (layer, 0, j))],
        out_specs=pl.BlockSpec((tm, tn), lambda i, j: (i, j)),
        out_shape=jax.ShapeDtypeStruct((T, N), F32),
        scratch_shapes=[pltpu.VMEM((tm, D), BF16)],
        compiler_params=_cparams(("parallel", "arbitrary")),
        name="norm_matmul",
    )(x, nw.reshape(1, D), w)


def _mm_res_kernel(*refs, nl):
    lhs = refs[:nl]
    w_ref, r_ref, o_ref = refs[nl], refs[nl + 1], refs[nl + 2]
    acc = r_ref[...]
    k0 = 0
    for l in lhs:
        k = l.shape[1]
        acc = acc + jnp.dot(l[...], w_ref[k0:k0 + k, :], preferred_element_type=F32)
        k0 += k
    o_ref[...] = acc


def matmul_residual(lhs_list, w, layer, res, *, tm=OUT_TM, tn=OUT_TN):
    T, N = res.shape
    K = w.shape[1]
    tm = min(tm, T)
    nl = len(lhs_list)
    in_specs = [pl.BlockSpec((tm, l.shape[1]), lambda j, i: (i, 0)) for l in lhs_list]
    in_specs += [pl.BlockSpec((None, K, tn), lambda j, i: (layer, 0, j)),
                 pl.BlockSpec((tm, tn), lambda j, i: (i, j))]
    return pl.pallas_call(
        functools.partial(_mm_res_kernel, nl=nl),
        grid=(N // tn, T // tm),
        in_specs=in_specs,
        out_specs=pl.BlockSpec((tm, tn), lambda j, i: (i, j)),
        out_shape=jax.ShapeDtypeStruct((T, N), F32),
        compiler_params=_cparams(("parallel", "parallel")),
        name="matmul_residual",
    )(*lhs_list, w, res)


def _proj_conv_kernel(*refs, nb, nsub, tm, seq, kw, rows, erows, aux):
    it = iter(refs)
    x_ref, nw_ref = next(it), next(it)
    w_refs = [next(it) for _ in range(nb)]
    cw_refs = [next(it) for _ in range(nb)]
    cb_refs = [next(it) for _ in range(nb)]
    aux_w = next(it) if aux else None
    o_ref = next(it)
    aux_o = next(it) if aux else None
    h_ref = next(it)
    pad = [[next(it) for _ in range(nsub)] for _ in range(nb)]
    halo = next(it)
    gate = [next(it) for _ in range(nsub)] if nb == 2 else None

    i = pl.program_id(0)
    j = pl.program_id(1)
    sub = o_ref.shape[1] // nsub

    @pl.when(j == 0)
    def _():
        _normalise_rows(x_ref, nw_ref, h_ref, rows)
        if aux:
            aux_o[...] = jnp.dot(h_ref[...], aux_w[...], preferred_element_type=F32)

    first = (i * tm) % seq == 0

    def product(s, b):
        cs = slice(s * sub, (s + 1) * sub)
        p = pad[b][s]
        p[0:8, :] = jnp.where(first, 0.0, halo[j, b, :, cs])
        p[8:8 + tm, :] = jnp.dot(h_ref[...], w_refs[b][:, cs], preferred_element_type=F32)
        halo[j, b, :, cs] = p[tm:tm + 8, :]

    def conv(s, b, r0):
        cs = slice(s * sub, (s + 1) * sub)
        blk = pad[b][s][r0:r0 + erows + 8, :]
        y = cb_refs[b][:, cs] + cw_refs[b][kw - 1:kw, cs] * blk[8:, :]
        for k in range(kw - 1):
            y = y + cw_refs[b][k:k + 1, cs] * pltpu.roll(blk, kw - 1 - k, axis=0)[8:, :]
        return y

    for s in range(nsub):
        cs = slice(s * sub, (s + 1) * sub)
        product(s, 0)
        for r0 in range(0, tm, erows):
            g = _silu(conv(s, 0, r0))
            if nb == 2:
                gate[s][r0:r0 + erows, :] = g
            else:
                o_ref[r0:r0 + erows, cs] = g.astype(o_ref.dtype)
        if nb == 2:
            product(s, 1)
            for r0 in range(0, tm, erows):
                o_ref[r0:r0 + erows, cs] = (gate[s][r0:r0 + erows, :] * conv(s, 1, r0)).astype(o_ref.dtype)


def proj_conv(x, nw, w, layer, cw, cb, conv_layer, *, nb, seq, tc, aux_w=None,
              tm=ROW_TILE, sub=CONV_SUB, block=CONV_BLOCK):
    T, D = x.shape
    kw = cw.shape[1]
    nj = cw.shape[2] // (nb * tc)
    tm = min(tm, seq)
    nsub = tc // sub
    in_specs = [pl.BlockSpec((tm, D), lambda i, j: (i, 0)), pl.BlockSpec((1, D), lambda i, j: (0, 0))]
    in_specs += [pl.BlockSpec((None, D, tc), functools.partial(lambda i, j, b: (layer, 0, j + b * nj), b=b))
                 for b in range(nb)]
    in_specs += [pl.BlockSpec((None, kw, tc),
                              functools.partial(lambda i, j, b: (conv_layer, 0, j + b * nj), b=b))
                 for b in range(nb)]
    in_specs += [pl.BlockSpec((None, 1, tc),
                              functools.partial(lambda i, j, b: (conv_layer, 0, j + b * nj), b=b))
                 for b in range(nb)]
    args = [x, nw.reshape(1, D)] + [w] * nb + [cw] * nb + [cb] * nb
    out_specs = [pl.BlockSpec((tm, tc), lambda i, j: (i, j))]
    out_shape = [jax.ShapeDtypeStruct((T, nj * tc), BF16)]
    aux = aux_w is not None
    if aux:
        in_specs.append(pl.BlockSpec(aux_w.shape, lambda i, j: (0, 0)))
        args.append(aux_w)
        out_specs.append(pl.BlockSpec((tm, aux_w.shape[1]), lambda i, j: (i, 0)))
        out_shape.append(jax.ShapeDtypeStruct((T, aux_w.shape[1]), F32))
    outs = pl.pallas_call(
        functools.partial(_proj_conv_kernel, nb=nb, nsub=nsub, tm=tm, seq=seq, kw=kw,
                          rows=NORM_ROWS, erows=min(block // sub, tm), aux=aux),
        grid=(T // tm, nj),
        in_specs=in_specs,
        out_specs=out_specs,
        out_shape=out_shape,
        scratch_shapes=([pltpu.VMEM((tm, D), BF16)]
                        + [pltpu.VMEM((tm + 8, sub), F32) for _ in range(nb * nsub)]
                        + [pltpu.VMEM((nj, nb, 8, tc), F32)]
                        + [pltpu.VMEM((tm, sub), F32) for _ in range(nsub if nb == 2 else 0)]),
        compiler_params=_cparams(("arbitrary", "arbitrary")),
        name="proj_conv",
    )(*args)
    return outs if aux else outs[0]


def _lru_kernel(xa_ref, ga_ref, cw_ref, cb_ref, wr_ref, br_ref, wi_ref, bi_ref, lam_ref, o_ref,
                xpad, a_s, u_s, hc, *, tt, kw):
    W = xa_ref.shape[1]

    @pl.when(pl.program_id(1) == 0)
    def _():
        xpad[0:8, :] = jnp.zeros((8, W), F32)
        hc[...] = jnp.zeros_like(hc)

    xpad[8:8 + tt, :] = xa_ref[...]
    full = xpad[...]
    xc = cb_ref[...] + cw_ref[kw - 1:kw, :] * full[8:, :]
    for k in range(kw - 1):
        xc = xc + cw_ref[k:k + 1, :] * pltpu.roll(full, kw - 1 - k, axis=0)[8:, :]
    xpad[0:8, :] = full[tt:tt + 8, :]

    xb = xc.astype(BF16)
    nt = W // V7X_MXU_DIM

    def gate(w_ref, b_ref):
        parts = [jnp.dot(xb[:, j * V7X_MXU_DIM:(j + 1) * V7X_MXU_DIM], w_ref[j],
                         preferred_element_type=F32) for j in range(nt)]
        return _sigmoid(jnp.concatenate(parts, axis=1) + b_ref[...])

    r = gate(wr_ref, br_ref)
    gi = gate(wi_ref, bi_ref)
    nl = -lam_ref[...]
    sp = jnp.maximum(nl, 0.0) + jnp.log1p(jnp.exp(-jnp.abs(nl)))
    log_a = (-LRU_C) * r * sp
    a_s[...] = jnp.exp(log_a)
    th = jnp.tanh(log_a)
    u_s[...] = jnp.sqrt(-2.0 * th / (1.0 - th)) * (gi * xc)

    rows = lax.broadcasted_iota(jnp.int32, (8, W), 0)

    def body(i, h):
        sl = pl.ds(pl.multiple_of(i * 8, 8), 8)
        a = a_s[sl, :]
        u = u_s[sl, :]
        for d in (1, 2, 4):
            m = rows >= d
            a_sh = pltpu.roll(a, d, axis=0)
            u_sh = pltpu.roll(u, d, axis=0)
            u = jnp.where(m, a * u_sh + u, u)
            a = jnp.where(m, a * a_sh, a)
        hh = a * h + u
        a_s[sl, :] = hh
        return hh[7:8, :]

    hc[0:1, :] = lax.fori_loop(0, tt // 8, body, hc[0:1, :])
    o_ref[...] = (a_s[...] * _gelu_tanh(ga_ref[...])).astype(o_ref.dtype)


def _blockdiag_tiles(w):
    nb, k, _ = w.shape
    per = V7X_MXU_DIM // k
    w4 = w.reshape(nb // per, per, k, k)
    eye = jnp.eye(per, dtype=w.dtype)
    t = jnp.einsum('tbij,bc->tbicj', w4, eye)
    return t.reshape(nb // per, V7X_MXU_DIM, V7X_MXU_DIM)


def lru_mixer(proj, cw, cb, w_r, b_r, w_i, b_i, lam, *, batch, seq, tt=256):
    T = proj.shape[0]
    kw, W = cw.shape
    tt = min(tt, seq)
    nt = seq // tt
    wr = _blockdiag_tiles(w_r).astype(BF16)
    wi = _blockdiag_tiles(w_i).astype(BF16)
    vec = lambda b, t: (0, 0)
    wspec = pl.BlockSpec(wr.shape, lambda b, t: (0, 0, 0))
    return pl.pallas_call(
        functools.partial(_lru_kernel, tt=tt, kw=kw),
        grid=(batch, nt),
        in_specs=[pl.BlockSpec((tt, W), lambda b, t: (b * nt + t, 0)),
                  pl.BlockSpec((tt, W), lambda b, t: (b * nt + t, 1)),
                  pl.BlockSpec((kw, W), vec), pl.BlockSpec((1, W), vec),
                  wspec, pl.BlockSpec((1, W), vec),
                  wspec, pl.BlockSpec((1, W), vec),
                  pl.BlockSpec((1, W), vec)],
        out_specs=pl.BlockSpec((tt, W), lambda b, t: (b * nt + t, 0)),
        out_shape=jax.ShapeDtypeStruct((T, W), BF16),
        scratch_shapes=[pltpu.VMEM((tt + 8, W), F32), pltpu.VMEM((tt, W), F32),
                        pltpu.VMEM((tt, W), F32), pltpu.VMEM((8, W), F32)],
        compiler_params=_cparams(("parallel", "arbitrary")),
        name="lru_mixer",
    )(proj, proj, cw, cb.reshape(1, W), wr, b_r.reshape(1, W), wi, b_i.reshape(1, W),
      lam.reshape(1, W))


def _hgrn2_kernel(q_ref, f_ref, v_ref, g_ref, lb_ref, nw_ref, o_ref, st, *, tt, layer, hb):
    C = HG_CHUNK

    @pl.when(pl.program_id(2) == 0)
    def _():
        st[...] = jnp.zeros_like(st)

    raw = lb_ref[...]
    e = jnp.exp(raw - jnp.max(raw, axis=0, keepdims=True))
    p = e / jnp.sum(e, axis=0, keepdims=True)
    lb_all = -p[0:1, :]
    for i in range(layer + 1):
        lb_all = lb_all + p[i:i + 1, :]

    r = lax.broadcasted_iota(jnp.int32, (C, C), 0)
    c = lax.broadcasted_iota(jnp.int32, (C, C), 1)
    causal = r >= c
    scale = HG_DK ** -0.5
    nw = nw_ref[...]

    nt_dims = (((1,), (1,)), ((), ()))
    tn_dims = (((0,), (0,)), ((), ()))
    heads = range(hb)
    cols = [slice(h * HG_DK, (h + 1) * HG_DK) for h in heads]

    def chunk(ci, carry):
        sl = pl.ds(pl.multiple_of(ci * C, C), C)
        ks, cums = [], []
        for h in heads:
            lb = lb_all[:, cols[h]]
            f = lb + (1.0 - lb) * _sigmoid(f_ref[sl, cols[h]])
            ks.append(1.0 - f)
            cums.append(_cumsum_rows(jnp.log(f)))
        qss, scs = [], []
        for h in heads:
            cum, k = cums[h], ks[h]
            mid = cum[C // 2 - 1:C // 2, :]
            qs = _silu(q_ref[sl, cols[h]]) * scale
            qd = (qs * jnp.exp(cum - mid)).astype(BF16)
            kd = (k * jnp.exp(mid - cum)).astype(BF16)
            qss.append(qs)
            scs.append(lax.dot_general(qd, kd, nt_dims, preferred_element_type=F32))
        outs = []
        for h in heads:
            cum, k = cums[h], ks[h]
            tot = cum[C - 1:C, :]
            vb = v_ref[sl, cols[h]].astype(BF16)
            sc = jnp.where(causal, scs[h], 0.0).astype(BF16)
            s_old = st[h]
            o = jnp.dot(sc, vb, preferred_element_type=F32)
            o = o + lax.dot_general((qss[h] * jnp.exp(cum)).astype(BF16), s_old.astype(BF16),
                                    nt_dims, preferred_element_type=F32)
            kdec = (k * jnp.exp(tot - cum)).astype(BF16)
            upd = lax.dot_general(vb, kdec, tn_dims, preferred_element_type=F32)
            st[h] = jnp.exp(tot) * s_old + upd
            outs.append(o)
        for h in heads:
            o = outs[h]
            o = o * lax.rsqrt(jnp.mean(o * o, axis=-1, keepdims=True) + EPS) * nw
            o_ref[sl, cols[h]] = (o * _silu(g_ref[sl, cols[h]])).astype(o_ref.dtype)
        return carry

    lax.fori_loop(0, tt // C, chunk, 0)


def hgrn2_mixer(proj, lower_bounds, norm_w, *, layer, batch, seq, col0, tt=256, hb=8):
    T = proj.shape[0]
    tt = min(tt, seq)
    nt = seq // tt
    H = HG_HEADS
    bw = hb * HG_DK
    c0 = col0 // bw
    nhb = H // hb

    def col(k):
        return pl.BlockSpec((tt, bw), lambda b, h, t: (b * nt + t, c0 + k * nhb + h))

    ne = lower_bounds.shape[0]
    return pl.pallas_call(
        functools.partial(_hgrn2_kernel, tt=tt, layer=layer, hb=hb),
        grid=(batch, nhb, nt),
        in_specs=[col(0), col(1), col(2), col(3),
                  pl.BlockSpec((ne, bw), lambda b, h, t: (0, h)),
                  pl.BlockSpec((1, HG_DK), lambda b, h, t: (0, 0))],
        out_specs=pl.BlockSpec((tt, bw), lambda b, h, t: (b * nt + t, h)),
        out_shape=jax.ShapeDtypeStruct((T, H * HG_DK), BF16),
        scratch_shapes=[pltpu.VMEM((hb, HG_DK, HG_DK), F32)],
        compiler_params=_cparams(("parallel", "parallel", "arbitrary")),
        name="hgrn2_mixer",
    )(proj, proj, proj, proj, lower_bounds, norm_w.reshape(1, HG_DK))


def _ssd_prep_kernel(dt_ref, bias_ref, alog_ref, cumg_ref, cumt_ref, wt_ref, etb_ref):
    L = SSD_CHUNK
    x = dt_ref[...] + bias_ref[...]
    dt = jnp.maximum(x, 0.0) + jnp.log1p(jnp.exp(-jnp.abs(x)))
    a_neg = -jnp.exp(alog_ref[...])
    cum = _cumsum_rows(dt * a_neg)
    col2 = cum * LOG2E
    for g in range(SSD_GROUPS):
        sh = (V7X_LANES - SSD_HPG * g) % V7X_LANES
        cumg_ref[:, g * V7X_LANES:(g + 1) * V7X_LANES] = col2 if sh == 0 else pltpu.roll(col2, sh, axis=1)
    cum_t = cum.T
    dt_t = dt.T
    tot = jnp.broadcast_to(cum_t[:, L - 1:L], cum_t.shape)
    cumt_ref[0] = (cum_t - jnp.log(dt_t)) * LOG2E
    wt_ref[0] = dt_t * jnp.exp(tot - cum_t)
    etb_ref[0] = jnp.exp(tot)


def ssd_prep(dt_raw, dt_bias, a_log):
    T = dt_raw.shape[0]
    L = SSD_CHUNK
    nc = T // L
    nh = dt_bias.shape[0]
    pad = V7X_LANES - nh
    bias = jnp.pad(dt_bias, (0, pad)).reshape(1, V7X_LANES)
    alog = jnp.pad(a_log, (0, pad)).reshape(1, V7X_LANES)
    sq = jax.ShapeDtypeStruct((nc, V7X_LANES, L), F32)
    sqspec = pl.BlockSpec((1, V7X_LANES, L), lambda i: (i, 0, 0))
    return pl.pallas_call(
        _ssd_prep_kernel,
        grid=(nc,),
        in_specs=[pl.BlockSpec((L, V7X_LANES), lambda i: (i, 0)),
                  pl.BlockSpec((1, V7X_LANES), lambda i: (0, 0)),
                  pl.BlockSpec((1, V7X_LANES), lambda i: (0, 0))],
        out_specs=[pl.BlockSpec((L, SSD_GROUPS * V7X_LANES), lambda i: (i, 0)),
                   sqspec, sqspec, sqspec],
        out_shape=[jax.ShapeDtypeStruct((T, SSD_GROUPS * V7X_LANES), F32), sq, sq, sq],
        compiler_params=_cparams(("parallel",)),
        name="ssd_prep",
    )(dt_raw, bias, alog)


def _ssd_kernel(z_ref, x_ref, b_ref, c_ref, cumg_ref, cumt_ref, wt_ref, etb_ref,
                d_ref, nw_ref, o_ref, st, *, tt, ng):
    L = SSD_CHUNK
    P2 = 2 * SSD_HEADDIM

    @pl.when(pl.program_id(2) == 0)
    def _():
        st[...] = jnp.zeros_like(st)

    r = lax.broadcasted_iota(jnp.int32, (L, L), 0)
    c = lax.broadcasted_iota(jnp.int32, (L, L), 1)
    causal = r >= c
    lo = lax.broadcasted_iota(jnp.int32, (L, P2), 1) < SSD_HEADDIM
    lo_row = lo[0:1, :]

    gw = SSD_HPG * SSD_HEADDIM
    N = SSD_STATE
    groups = range(ng)

    def chunk(ci, carry):
        sl = pl.ds(pl.multiple_of(ci * L, L), L)
        cum_all, w_all, et_all = cumt_ref[ci], wt_ref[ci], etb_ref[ci]
        xs, bts, cbms, olds, yoffs, cumgs, ys = [], [], [], [], [], [], [[] for _ in groups]
        for gi in groups:
            b16 = b_ref[sl, gi * N:(gi + 1) * N]
            c16 = c_ref[sl, gi * N:(gi + 1) * N]
            xs.append(x_ref[sl, gi * gw:(gi + 1) * gw].astype(F32))
            bts.append(b16.astype(F32).T)
            cbms.append(lax.dot_general(c16, b16, (((1,), (1,)), ((), ())),
                                        preferred_element_type=F32))
            olds.append(st[gi])
            yoffs.append(jnp.dot(c16, olds[gi].astype(BF16), preferred_element_type=F32))
            cumgs.append(cumg_ref[sl, gi * V7X_LANES:(gi + 1) * V7X_LANES])
        for p in range(SSD_HPG // 2):
            for gi in groups:
                ms, ws, cols = [], [], []
                for j in (2 * p, 2 * p + 1):
                    row = gi * SSD_HPG + j
                    col = jnp.broadcast_to(cumgs[gi][:, j:j + 1], (L, L))
                    seg = jnp.where(causal, col - cum_all[row:row + 1, :], NEG_BIG)
                    ms.append(cbms[gi] * jnp.exp2(seg))
                    ws.append(bts[gi] * w_all[row:row + 1, :])
                    cols.append(col)
                lhs = jnp.concatenate([jnp.concatenate(ms, axis=1), jnp.concatenate(ws, axis=1)],
                                      axis=0).astype(BF16)
                xp = xs[gi][:, p * P2:(p + 1) * P2]
                x2 = jnp.concatenate([jnp.where(lo, xp, 0.0), jnp.where(lo, 0.0, xp)],
                                     axis=0).astype(BF16)
                res = jnp.dot(lhs, x2, preferred_element_type=F32)
                ecum = jnp.exp2(jnp.where(lo, cols[0], cols[1]))
                ys[gi].append(res[:L] + yoffs[gi][:, p * P2:(p + 1) * P2] * ecum)
                r0 = gi * SSD_HPG + 2 * p
                et = jnp.where(lo_row, et_all[r0:r0 + 1, :], et_all[r0 + 1:r0 + 2, :])
                st[gi, :, p * P2:(p + 1) * P2] = et * olds[gi][:, p * P2:(p + 1) * P2] + res[L:]
        for gi in groups:
            gs = slice(gi * gw, (gi + 1) * gw)
            y = jnp.concatenate(ys[gi], axis=1) + xs[gi] * d_ref[:, gs]
            y = y * z_ref[sl, gs].astype(F32)
            y = y * lax.rsqrt(jnp.mean(y * y, axis=-1, keepdims=True) + EPS) * nw_ref[:, gs]
            o_ref[sl, gs] = y.astype(o_ref.dtype)
        return carry

    lax.fori_loop(0, tt // L, chunk, 0)


def ssd_mixer(act, prep, d_skip, norm_w, *, batch, seq, tt=512):
    T = act.shape[0]
    cumg, cumt, wt, etb = prep
    G = SSD_GROUPS
    gw = SSD_HPG * SSD_HEADDIM
    inner = G * gw
    N = SSD_STATE
    L = SSD_CHUNK
    tt = min(tt, seq)
    nt = seq // tt
    ncb = tt // L
    ng = SSD_GROUPS_PER_STEP
    row = lambda b, g, t: b * nt + t
    bw, sw = ng * gw, ng * N
    xo, bo, co = inner // bw, 2 * inner // sw, (2 * inner + G * N) // sw
    d_exp = jnp.repeat(d_skip, SSD_HEADDIM).reshape(1, inner)
    sq = lambda: pl.BlockSpec((ncb, ng * SSD_HPG, L), lambda b, g, t: (row(b, g, t), g, 0))
    return pl.pallas_call(
        functools.partial(_ssd_kernel, tt=tt, ng=ng),
        grid=(batch, G // ng, nt),
        in_specs=[pl.BlockSpec((tt, bw), lambda b, g, t: (row(b, g, t), g)),
                  pl.BlockSpec((tt, bw), lambda b, g, t: (row(b, g, t), xo + g)),
                  pl.BlockSpec((tt, sw), lambda b, g, t: (row(b, g, t), bo + g)),
                  pl.BlockSpec((tt, sw), lambda b, g, t: (row(b, g, t), co + g)),
                  pl.BlockSpec((tt, ng * V7X_LANES), lambda b, g, t: (row(b, g, t), g)),
                  sq(), sq(), sq(),
                  pl.BlockSpec((1, bw), lambda b, g, t: (0, g)),
                  pl.BlockSpec((1, bw), lambda b, g, t: (0, g))],
        out_specs=pl.BlockSpec((tt, bw), lambda b, g, t: (row(b, g, t), g)),
        out_shape=jax.ShapeDtypeStruct((T, inner), BF16),
        scratch_shapes=[pltpu.VMEM((ng, N, gw), F32)],
        compiler_params=_cparams(("parallel", "parallel", "arbitrary")),
        name="ssd_mixer",
    )(act, act, act, act, cumg, cumt, wt, etb, d_exp, norm_w.reshape(1, inner))


def _rmsnorm_kernel(x_ref, w_ref, o_ref):
    x = x_ref[...]
    ms = jnp.mean(x * x, axis=-1, keepdims=True)
    o_ref[...] = x * lax.rsqrt(ms + EPS) * w_ref[...]


def rmsnorm(x, w, *, tm=256):
    T, D = x.shape
    return pl.pallas_call(
        _rmsnorm_kernel,
        grid=(T // tm,),
        in_specs=[pl.BlockSpec((tm, D), lambda i: (i, 0)), pl.BlockSpec((1, D), lambda i: (0, 0))],
        out_specs=pl.BlockSpec((tm, D), lambda i: (i, 0)),
        out_shape=jax.ShapeDtypeStruct((T, D), F32),
        compiler_params=_cparams(("parallel",)),
        name="final_rmsnorm",
    )(x, w.reshape(1, D))


def even_layer(x, nw, w_in, w_out, e, lru_cw, lru_cb, w_r, b_r, w_i, b_i, lam, lower_bounds, hg_nw,
               *, batch, seq):
    lru_w = lru_cw.shape[1]
    proj = norm_matmul(x, nw, w_in, e)
    ya = lru_mixer(proj, lru_cw, lru_cb, w_r, b_r, w_i, b_i, lam, batch=batch, seq=seq)
    yb = hgrn2_mixer(proj, lower_bounds, hg_nw, layer=e, batch=batch, seq=seq, col0=2 * lru_w)
    return matmul_residual([ya, yb], w_out, e, x, tn=w_out.shape[2])


def odd_layer(x, nw, w_in, w_out, o, w_dt, conv_w, conv_b, dt_bias, a_log, d_skip, norm_w,
              *, batch, seq):
    kw, conv_dim = conv_w.shape
    inner = norm_w.shape[0]
    nh = dt_bias.shape[0]
    ident = jnp.zeros((kw, inner), F32).at[kw - 1].set(1.0)
    cw = jnp.concatenate([ident, conv_w], axis=1)[None]
    cb = jnp.concatenate([jnp.zeros((inner,), F32), conv_b]).reshape(1, 1, inner + conv_dim)
    aux_w = jnp.pad(w_dt, ((0, 0), (0, V7X_LANES - nh))).astype(BF16)
    act, dt_raw = proj_conv(x, nw, w_in, o, cw, cb, 0, nb=1, seq=seq, tc=IN_PROJ_TN, aux_w=aux_w)
    prep = ssd_prep(dt_raw, dt_bias, a_log)
    y = ssd_mixer(act, prep, d_skip, norm_w, batch=batch, seq=seq)
    return matmul_residual([y], w_out, o, x)


def ffn_layer(x, nw, w_up, w_down, l, conv_w, conv_b, *, seq):
    block = CONV_BLOCK if l < 2 else CONV_BLOCK // 16
    act = proj_conv(x, nw, w_up, l, conv_w, conv_b, l, nb=2, seq=seq, tc=FFN_TC, block=block)
    return matmul_residual([act], w_down, l, x)


def kernel(x, norm_mix_w, norm_ffn_w, norm_final_w, ev_w_in, lru_conv_w, lru_conv_b, lru_w_r, lru_b_r,
           lru_w_i, lru_b_i, lru_lambda, hg_lower_bounds, hg_norm_w, ev_w_out, ssd_w_in, ssd_conv_w,
           ssd_conv_b, ssd_dt_bias, ssd_a_log, ssd_d, ssd_norm_w, ssd_w_out, ffn_w_up, ffn_conv_w,
           ffn_conv_b, ffn_w_down):
    batch, seq, d = x.shape
    depth = norm_mix_w.shape[0]
    nh = ssd_dt_bias.shape[1]
    ev_in, ev_out = cast_bf16(ev_w_in), cast_bf16(ev_w_out)
    ssd_in = cast_bf16(ssd_w_in, ncols=ssd_w_in.shape[2] - nh)
    ssd_out = cast_bf16(ssd_w_out)
    up, down = cast_bf16(ffn_w_up), cast_bf16(ffn_w_down)
    ffn_cb = ffn_conv_b.reshape(depth, 1, -1)
    h = x.reshape(batch * seq, d)
    for l in range(depth):
        if l % 2 == 0:
            e = l // 2
            h = even_layer(h, norm_mix_w[l], ev_in, ev_out, e, lru_conv_w[e], lru_conv_b[e], lru_w_r[e],
                           lru_b_r[e], lru_w_i[e], lru_b_i[e], lru_lambda[e], hg_lower_bounds,
                           hg_norm_w[e], batch=batch, seq=seq)
        else:
            o = l // 2
            h = odd_layer(h, norm_mix_w[l], ssd_in, ssd_out, o, ssd_w_in[o][:, -nh:], ssd_conv_w[o],
                          ssd_conv_b[o], ssd_dt_bias[o], ssd_a_log[o], ssd_d[o], ssd_norm_w[o],
                          batch=batch, seq=seq)
        h = ffn_layer(h, norm_ffn_w[l], up, down, l, ffn_conv_w, ffn_cb, seq=seq)
    return rmsnorm(h, norm_final_w).reshape(batch, seq, d)
```

```python
import functools

import jax
import jax.numpy as jnp
from jax import lax
from jax.experimental import pallas as pl
from jax.experimental.pallas import tpu as pltpu

F32 = jnp.float32
BF16 = jnp.bfloat16
EPS = 1e-6

V7X_LANES = 128
V7X_SUBLANES = 8
V7X_MXU_DIM = 256
V7X_VMEM_BYTES = 64 * 1024 * 1024
VMEM_LIMIT = V7X_VMEM_BYTES - 8 * 1024 * 1024

LRU_BLOCKS = 16
LRU_C = 8.0
HG_HEADS = 8
HG_DK = 128
HG_CHUNK = 64
SSD_HEADDIM = 64
SSD_GROUPS = 8
SSD_HPG = 8
SSD_STATE = 128
SSD_CHUNK = 128
SSD_GROUPS_PER_STEP = 2
NEG_BIG = -1e30
LOG2E = 1.4426950408889634

ROW_TILE = 1024
NORM_ROWS = 128
IN_PROJ_TN = 1024
FFN_TC = 512
CONV_SUB = V7X_MXU_DIM
CONV_BLOCK = 1024 * 256
OUT_TM = 512
OUT_TN = 1024


def _cparams(sem):
    return pltpu.CompilerParams(dimension_semantics=sem, vmem_limit_bytes=VMEM_LIMIT)


def _sigmoid(x):
    return 1.0 / (1.0 + jnp.exp(-x))


def _silu(x):
    return x * _sigmoid(x)


def _gelu_tanh(x):
    return 0.5 * x * (1.0 + jnp.tanh(0.7978845608028654 * (x + 0.044715 * (x * x * x))))


def _split3(x):
    hi = x.astype(BF16)
    r1 = x - hi.astype(F32)
    mid = r1.astype(BF16)
    lo = (r1 - mid.astype(F32)).astype(BF16)
    return jnp.concatenate([hi, mid, lo], axis=1)


def _cumsum_rows(x):
    n, w = x.shape
    r = lax.broadcasted_iota(jnp.int32, (n, n), 0)
    c = lax.broadcasted_iota(jnp.int32, (n, n), 1)
    tril = jnp.where(r >= c, 1.0, 0.0).astype(BF16)
    p = jnp.dot(tril, _split3(x), preferred_element_type=F32)
    return p[:, :w] + p[:, w:2 * w] + p[:, 2 * w:]


def _normalise_rows(x_ref, nw_ref, h_ref, rows):
    def body(r, c):
        sl = pl.ds(pl.multiple_of(r * rows, rows), rows)
        x = x_ref[sl, :]
        ms = jnp.mean(x * x, axis=-1, keepdims=True)
        h_ref[sl, :] = (x * lax.rsqrt(ms + EPS) * nw_ref[...]).astype(BF16)
        return c
    lax.fori_loop(0, x_ref.shape[0] // rows, body, 0)


def _cast_kernel(x_ref, o_ref):
    o_ref[...] = x_ref[...].astype(o_ref.dtype)


def cast_bf16(w, ncols=None):
    L, K, _ = w.shape
    N = w.shape[2] if ncols is None else ncols
    tk = 512
    tn = max(t for t in range(V7X_LANES, 2816 + 1, V7X_LANES) if N % t == 0)
    return pl.pallas_call(
        _cast_kernel,
        grid=(L, K // tk, N // tn),
        in_specs=[pl.BlockSpec((1, tk, tn), lambda l, i, j: (l, i, j))],
        out_specs=pl.BlockSpec((1, tk, tn), lambda l, i, j: (l, i, j)),
        out_shape=jax.ShapeDtypeStruct((L, K, N), BF16),
        compiler_params=_cparams(("parallel", "parallel", "parallel")),
        name="cast_bf16",
    )(w)


def _norm_mm_kernel(x_ref, nw_ref, w_ref, o_ref, h_ref, *, rows):
    @pl.when(pl.program_id(1) == 0)
    def _():
        _normalise_rows(x_ref, nw_ref, h_ref, rows)

    o_ref[...] = jnp.dot(h_ref[...], w_ref[...], preferred_element_type=F32).astype(o_ref.dtype)


def norm_matmul(x, nw, w, layer, *, tm=ROW_TILE, tn=IN_PROJ_TN):
    T, D = x.shape
    N = w.shape[2]
    tm = min(tm, T)
    return pl.pallas_call(
        functools.partial(_norm_mm_kernel, rows=NORM_ROWS),
        grid=(T // tm, N // tn),
        in_specs=[pl.BlockSpec((tm, D), lambda i, j: (i, 0)),
                  pl.BlockSpec((1, D), lambda i, j: (0, 0)),
                  pl.BlockSpec((None, D, tn), lambda i, j: (layer, 0, j))],
        out_specs=pl.BlockSpec((tm, tn), lambda i, j: (i, j)),
        out_shape=jax.ShapeDtypeStruct((T, N), F32),
        scratch_shapes=[pltpu.VMEM((tm, D), BF16)],
        compiler_params=_cparams(("parallel", "arbitrary")),
        name="norm_matmul",
    )(x, nw.reshape(1, D), w)


def _mm_res_kernel(*refs, nl):
    lhs = refs[:nl]
    w_ref, r_ref, o_ref = refs[nl], refs[nl + 1], refs[nl + 2]
    acc = r_ref[...]
    k0 = 0
    for l in lhs:
        k = l.shape[1]
        acc = acc + jnp.dot(l[...], w_ref[k0:k0 + k, :], preferred_element_type=F32)
        k0 += k
    o_ref[...] = acc


def matmul_residual(lhs_list, w, layer, res, *, tm=OUT_TM, tn=OUT_TN):
    T, N = res.shape
    K = w.shape[1]
    tm = min(tm, T)
    nl = len(lhs_list)
    in_specs = [pl.BlockSpec((tm, l.shape[1]), lambda j, i: (i, 0)) for l in lhs_list]
    in_specs += [pl.BlockSpec((None, K, tn), lambda j, i: (layer, 0, j)),
                 pl.BlockSpec((tm, tn), lambda j, i: (i, j))]
    return pl.pallas_call(
        functools.partial(_mm_res_kernel, nl=nl),
        grid=(N // tn, T // tm),
        in_specs=in_specs,
        out_specs=pl.BlockSpec((tm, tn), lambda j, i: (i, j)),
        out_shape=jax.ShapeDtypeStruct((T, N), F32),
        compiler_params=_cparams(("parallel", "parallel")),
        name="matmul_residual",
    )(*lhs_list, w, res)


def _proj_conv_kernel(*refs, nb, nsub, tm, seq, kw, rows, erows, aux):
    it = iter(refs)
    x_ref, nw_ref = next(it), next(it)
    w_refs = [next(it) for _ in range(nb)]
    cw_refs = [next(it) for _ in range(nb)]
    cb_refs = [next(it) for _ in range(nb)]
    aux_w = next(it) if aux else None
    o_ref = next(it)
    aux_o = next(it) if aux else None
    h_ref = next(it)
    pads = [next(it) for _ in range(nb)]
    pad = [[pads[b].at[s] for s in range(nsub)] for b in range(nb)]
    halo = next(it)

    i = pl.program_id(0)
    j = pl.program_id(1)
    sub = o_ref.shape[1] // nsub

    @pl.when(j == 0)
    def _():
        _normalise_rows(x_ref, nw_ref, h_ref, rows)
        if aux:
            aux_o[...] = jnp.dot(h_ref[...], aux_w[...], preferred_element_type=F32)

    first = (i * tm) % seq == 0

    def product(s, b):
        cs = slice(s * sub, (s + 1) * sub)
        p = pad[b][s]
        p[0:8, :] = jnp.where(first, 0.0, halo[j, b, :, cs])
        p[8:8 + tm, :] = jnp.dot(h_ref[...], w_refs[b][:, cs], preferred_element_type=F32)
        halo[j, b, :, cs] = p[tm:tm + 8, :]

    def conv(s, b, r0):
        cs = slice(s * sub, (s + 1) * sub)
        blk = pad[b][s][r0:r0 + erows + 8, :]
        y = cb_refs[b][:, cs] + cw_refs[b][kw - 1:kw, cs] * blk[8:, :]
        for k in range(kw - 1):
            y = y + cw_refs[b][k:k + 1, cs] * pltpu.roll(blk, kw - 1 - k, axis=0)[8:, :]
        return y

    for s in range(nsub):
        cs = slice(s * sub, (s + 1) * sub)
        product(s, 0)
        if nb == 2:
            yg = conv(s, 0, 0)
            product(s, 1)
            o_ref[:, cs] = (_silu(yg) * conv(s, 1, 0)).astype(o_ref.dtype)
        else:
            for r0 in range(0, tm, erows):
                o_ref[r0:r0 + erows, cs] = _silu(conv(s, 0, r0)).astype(o_ref.dtype)


def proj_conv(x, nw, w, layer, cw, cb, conv_layer, *, nb, seq, tc, aux_w=None,
              tm=ROW_TILE, sub=CONV_SUB, block=CONV_BLOCK):
    T, D = x.shape
    kw = cw.shape[1]
    nj = cw.shape[2] // (nb * tc)
    tm = min(tm, seq)
    nsub = tc // sub
    in_specs = [pl.BlockSpec((tm, D), lambda i, j: (i, 0)), pl.BlockSpec((1, D), lambda i, j: (0, 0))]
    in_specs += [pl.BlockSpec((None, D, tc), functools.partial(lambda i, j, b: (layer, 0, j + b * nj), b=b))
                 for b in range(nb)]
    in_specs += [pl.BlockSpec((None, kw, tc),
                              functools.partial(lambda i, j, b: (conv_layer, 0, j + b * nj), b=b))
                 for b in range(nb)]
    in_specs += [pl.BlockSpec((None, 1, tc),
                              functools.partial(lambda i, j, b: (conv_layer, 0, j + b * nj), b=b))
                 for b in range(nb)]
    args = [x, nw.reshape(1, D)] + [w] * nb + [cw] * nb + [cb] * nb
    out_specs = [pl.BlockSpec((tm, tc), lambda i, j: (i, j))]
    out_shape = [jax.ShapeDtypeStruct((T, nj * tc), BF16)]
    aux = aux_w is not None
    if aux:
        in_specs.append(pl.BlockSpec(aux_w.shape, lambda i, j: (0, 0)))
        args.append(aux_w)
        out_specs.append(pl.BlockSpec((tm, aux_w.shape[1]), lambda i, j: (i, 0)))
        out_shape.append(jax.ShapeDtypeStruct((T, aux_w.shape[1]), F32))
    outs = pl.pallas_call(
        functools.partial(_proj_conv_kernel, nb=nb, nsub=nsub, tm=tm, seq=seq, kw=kw,
                          rows=NORM_ROWS, erows=min(block // sub, tm), aux=aux),
        grid=(T // tm, nj),
        in_specs=in_specs,
        out_specs=out_specs,
        out_shape=out_shape,
        scratch_shapes=([pltpu.VMEM((tm, D), BF16)]
                        + [pltpu.VMEM((nsub, tm + 8, sub), F32) for _ in range(nb)]
                        + [pltpu.VMEM((nj, nb, 8, tc), F32)]),
        compiler_params=_cparams(("arbitrary", "arbitrary")),
        name="proj_conv",
    )(*args)
    return outs if aux else outs[0]


def _ffn_up_kernel(x_ref, nw_ref, wg_ref, wv_ref, cwg_ref, cwv_ref, cbg_ref, cbv_ref, o_ref,
                   h_ref, gpad, vpad, halo, *, tm, seq, kw, rows, sub):
    i = pl.program_id(0)
    j = pl.program_id(1)

    @pl.when(j == 0)
    def _():
        _normalise_rows(x_ref, nw_ref, h_ref, rows)

    first = (i * tm) % seq == 0
    tc = o_ref.shape[1]

    def branch(w_ref, pad, cw_ref, cb_ref, slot, s):
        cs = slice(s * sub, (s + 1) * sub)
        pad[s, 0:8, :] = jnp.where(first, 0.0, halo[j, slot, :, cs])
        pad[s, 8:8 + tm, :] = jnp.dot(h_ref[...], w_ref[:, cs], preferred_element_type=F32)
        halo[j, slot, :, cs] = pad[s, tm:tm + 8, :]
        y = cb_ref[:, cs] + cw_ref[kw - 1:kw, cs] * pad[s, 8:8 + tm, :]
        for k in range(kw - 1):
            y = y + cw_ref[k:k + 1, cs] * pad[s, pl.ds(8 - (kw - 1) + k, tm), :]
        return y

    for s in range(tc // sub):
        g = branch(wg_ref, gpad, cwg_ref, cbg_ref, 0, s)
        v = branch(wv_ref, vpad, cwv_ref, cbv_ref, 1, s)
        o_ref[:, s * sub:(s + 1) * sub] = (_silu(g) * v).astype(o_ref.dtype)


def ffn_up(x, nw, w_up, layer, cw, cb, *, seq, tm=ROW_TILE, tc=FFN_TC, sub=CONV_SUB):
    T, D = x.shape
    F = w_up.shape[2] // 2
    kw = cw.shape[1]
    tm = min(tm, seq)
    nj = F // tc
    wspec = lambda off: pl.BlockSpec((None, D, tc), lambda i, j: (layer, 0, j + off))
    cwspec = lambda off: pl.BlockSpec((None, kw, tc), lambda i, j: (layer, 0, j + off))
    cbspec = lambda off: pl.BlockSpec((None, 1, tc), lambda i, j: (layer, 0, j + off))
    return pl.pallas_call(
        functools.partial(_ffn_up_kernel, tm=tm, seq=seq, kw=kw, rows=NORM_ROWS, sub=sub),
        grid=(T // tm, nj),
        in_specs=[pl.BlockSpec((tm, D), lambda i, j: (i, 0)),
                  pl.BlockSpec((1, D), lambda i, j: (0, 0)),
                  wspec(0), wspec(nj), cwspec(0), cwspec(nj), cbspec(0), cbspec(nj)],
        out_specs=pl.BlockSpec((tm, tc), lambda i, j: (i, j)),
        out_shape=jax.ShapeDtypeStruct((T, F), BF16),
        scratch_shapes=[pltpu.VMEM((tm, D), BF16),
                        pltpu.VMEM((tc // sub, tm + 8, sub), F32),
                        pltpu.VMEM((tc // sub, tm + 8, sub), F32),
                        pltpu.VMEM((nj, 2, 8, tc), F32)],
        compiler_params=_cparams(("arbitrary", "arbitrary")),
        name="ffn_up",
    )(x, nw.reshape(1, D), w_up, w_up, cw, cw, cb, cb)


def _lru_kernel(xa_ref, ga_ref, cw_ref, cb_ref, wr_ref, br_ref, wi_ref, bi_ref, lam_ref, o_ref,
                xpad, a_s, u_s, hc, *, tt, kw):
    W = xa_ref.shape[1]

    @pl.when(pl.program_id(1) == 0)
    def _():
        xpad[0:8, :] = jnp.zeros((8, W), F32)
        hc[...] = jnp.zeros_like(hc)

    xpad[8:8 + tt, :] = xa_ref[...]
    full = xpad[...]
    xc = cb_ref[...] + cw_ref[kw - 1:kw, :] * full[8:, :]
    for k in range(kw - 1):
        xc = xc + cw_ref[k:k + 1, :] * pltpu.roll(full, kw - 1 - k, axis=0)[8:, :]
    xpad[0:8, :] = full[tt:tt + 8, :]

    xb = xc.astype(BF16)
    nt = W // V7X_MXU_DIM

    def gate(w_ref, b_ref):
        parts = [jnp.dot(xb[:, j * V7X_MXU_DIM:(j + 1) * V7X_MXU_DIM], w_ref[j],
                         preferred_element_type=F32) for j in range(nt)]
        return _sigmoid(jnp.concatenate(parts, axis=1) + b_ref[...])

    r = gate(wr_ref, br_ref)
    gi = gate(wi_ref, bi_ref)
    nl = -lam_ref[...]
    sp = jnp.maximum(nl, 0.0) + jnp.log1p(jnp.exp(-jnp.abs(nl)))
    log_a = (-LRU_C) * r * sp
    a_s[...] = jnp.exp(log_a)
    th = jnp.tanh(log_a)
    u_s[...] = jnp.sqrt(-2.0 * th / (1.0 - th)) * (gi * xc)

    rows = lax.broadcasted_iota(jnp.int32, (8, W), 0)

    def body(i, h):
        sl = pl.ds(pl.multiple_of(i * 8, 8), 8)
        a = a_s[sl, :]
        u = u_s[sl, :]
        for d in (1, 2, 4):
            m = rows >= d
            a_sh = pltpu.roll(a, d, axis=0)
            u_sh = pltpu.roll(u, d, axis=0)
            u = jnp.where(m, a * u_sh + u, u)
            a = jnp.where(m, a * a_sh, a)
        hh = a * h + u
        a_s[sl, :] = hh
        return hh[7:8, :]

    hc[0:1, :] = lax.fori_loop(0, tt // 8, body, hc[0:1, :])
    o_ref[...] = (a_s[...] * _gelu_tanh(ga_ref[...])).astype(o_ref.dtype)


def _blockdiag_tiles(w):
    nb, k, _ = w.shape
    per = V7X_MXU_DIM // k
    w4 = w.reshape(nb // per, per, k, k)
    eye = jnp.eye(per, dtype=w.dtype)
    t = jnp.einsum('tbij,bc->tbicj', w4, eye)
    return t.reshape(nb // per, V7X_MXU_DIM, V7X_MXU_DIM)


def lru_mixer(proj, cw, cb, w_r, b_r, w_i, b_i, lam, *, batch, seq, tt=256):
    T = proj.shape[0]
    kw, W = cw.shape
    tt = min(tt, seq)
    nt = seq // tt
    wr = _blockdiag_tiles(w_r).astype(BF16)
    wi = _blockdiag_tiles(w_i).astype(BF16)
    vec = lambda b, t: (0, 0)
    wspec = pl.BlockSpec(wr.shape, lambda b, t: (0, 0, 0))
    return pl.pallas_call(
        functools.partial(_lru_kernel, tt=tt, kw=kw),
        grid=(batch, nt),
        in_specs=[pl.BlockSpec((tt, W), lambda b, t: (b * nt + t, 0)),
                  pl.BlockSpec((tt, W), lambda b, t: (b * nt + t, 1)),
                  pl.BlockSpec((kw, W), vec), pl.BlockSpec((1, W), vec),
                  wspec, pl.BlockSpec((1, W), vec),
                  wspec, pl.BlockSpec((1, W), vec),
                  pl.BlockSpec((1, W), vec)],
        out_specs=pl.BlockSpec((tt, W), lambda b, t: (b * nt + t, 0)),
        out_shape=jax.ShapeDtypeStruct((T, W), BF16),
        scratch_shapes=[pltpu.VMEM((tt + 8, W), F32), pltpu.VMEM((tt, W), F32),
                        pltpu.VMEM((tt, W), F32), pltpu.VMEM((8, W), F32)],
        compiler_params=_cparams(("parallel", "arbitrary")),
        name="lru_mixer",
    )(proj, proj, cw, cb.reshape(1, W), wr, b_r.reshape(1, W), wi, b_i.reshape(1, W),
      lam.reshape(1, W))


def _hgrn2_kernel(q_ref, f_ref, v_ref, g_ref, lb_ref, nw_ref, o_ref, st, *, tt, layer, hb):
    C = HG_CHUNK

    @pl.when(pl.program_id(2) == 0)
    def _():
        st[...] = jnp.zeros_like(st)

    raw = lb_ref[...]
    e = jnp.exp(raw - jnp.max(raw, axis=0, keepdims=True))
    p = e / jnp.sum(e, axis=0, keepdims=True)
    lb_all = -p[0:1, :]
    for i in range(layer + 1):
        lb_all = lb_all + p[i:i + 1, :]

    r = lax.broadcasted_iota(jnp.int32, (C, C), 0)
    c = lax.broadcasted_iota(jnp.int32, (C, C), 1)
    causal = r >= c
    scale = HG_DK ** -0.5
    nw = nw_ref[...]

    nt_dims = (((1,), (1,)), ((), ()))
    tn_dims = (((0,), (0,)), ((), ()))
    heads = range(hb)
    cols = [slice(h * HG_DK, (h + 1) * HG_DK) for h in heads]

    def chunk(ci, carry):
        sl = pl.ds(pl.multiple_of(ci * C, C), C)
        ks, cums = [], []
        for h in heads:
            lb = lb_all[:, cols[h]]
            f = lb + (1.0 - lb) * _sigmoid(f_ref[sl, cols[h]])
            ks.append(1.0 - f)
            cums.append(_cumsum_rows(jnp.log(f)))
        qss, scs = [], []
        for h in heads:
            cum, k = cums[h], ks[h]
            mid = cum[C // 2 - 1:C // 2, :]
            qs = _silu(q_ref[sl, cols[h]]) * scale
            qd = (qs * jnp.exp(cum - mid)).astype(BF16)
            kd = (k * jnp.exp(mid - cum)).astype(BF16)
            qss.append(qs)
            scs.append(lax.dot_general(qd, kd, nt_dims, preferred_element_type=F32))
        outs = []
        for h in heads:
            cum, k = cums[h], ks[h]
            tot = cum[C - 1:C, :]
            vb = v_ref[sl, cols[h]].astype(BF16)
            sc = jnp.where(causal, scs[h], 0.0).astype(BF16)
            s_old = st[h]
            o = jnp.dot(sc, vb, preferred_element_type=F32)
            o = o + lax.dot_general((qss[h] * jnp.exp(cum)).astype(BF16), s_old.astype(BF16),
                                    nt_dims, preferred_element_type=F32)
            kdec = (k * jnp.exp(tot - cum)).astype(BF16)
            upd = lax.dot_general(vb, kdec, tn_dims, preferred_element_type=F32)
            st[h] = jnp.exp(tot) * s_old + upd
            outs.append(o)
        for h in heads:
            o = outs[h]
            o = o * lax.rsqrt(jnp.mean(o * o, axis=-1, keepdims=True) + EPS) * nw
            o_ref[sl, cols[h]] = (o * _silu(g_ref[sl, cols[h]])).astype(o_ref.dtype)
        return carry

    lax.fori_loop(0, tt // C, chunk, 0)


def hgrn2_mixer(proj, lower_bounds, norm_w, *, layer, batch, seq, col0, tt=256, hb=8):
    T = proj.shape[0]
    tt = min(tt, seq)
    nt = seq // tt
    H = HG_HEADS
    bw = hb * HG_DK
    c0 = col0 // bw
    nhb = H // hb

    def col(k):
        return pl.BlockSpec((tt, bw), lambda b, h, t: (b * nt + t, c0 + k * nhb + h))

    ne = lower_bounds.shape[0]
    return pl.pallas_call(
        functools.partial(_hgrn2_kernel, tt=tt, layer=layer, hb=hb),
        grid=(batch, nhb, nt),
        in_specs=[col(0), col(1), col(2), col(3),
                  pl.BlockSpec((ne, bw), lambda b, h, t: (0, h)),
                  pl.BlockSpec((1, HG_DK), lambda b, h, t: (0, 0))],
        out_specs=pl.BlockSpec((tt, bw), lambda b, h, t: (b * nt + t, h)),
        out_shape=jax.ShapeDtypeStruct((T, H * HG_DK), BF16),
        scratch_shapes=[pltpu.VMEM((hb, HG_DK, HG_DK), F32)],
        compiler_params=_cparams(("parallel", "parallel", "arbitrary")),
        name="hgrn2_mixer",
    )(proj, proj, proj, proj, lower_bounds, norm_w.reshape(1, HG_DK))


def _ssd_prep_kernel(dt_ref, bias_ref, alog_ref, cumg_ref, cumt_ref, wt_ref, etb_ref):
    L = SSD_CHUNK
    x = dt_ref[...] + bias_ref[...]
    dt = jnp.maximum(x, 0.0) + jnp.log1p(jnp.exp(-jnp.abs(x)))
    a_neg = -jnp.exp(alog_ref[...])
    cum = _cumsum_rows(dt * a_neg)
    col2 = cum * LOG2E
    for g in range(SSD_GROUPS):
        sh = (V7X_LANES - SSD_HPG * g) % V7X_LANES
        cumg_ref[:, g * V7X_LANES:(g + 1) * V7X_LANES] = col2 if sh == 0 else pltpu.roll(col2, sh, axis=1)
    cum_t = cum.T
    dt_t = dt.T
    tot = jnp.broadcast_to(cum_t[:, L - 1:L], cum_t.shape)
    cumt_ref[0] = (cum_t - jnp.log(dt_t)) * LOG2E
    wt_ref[0] = dt_t * jnp.exp(tot - cum_t)
    etb_ref[0] = jnp.exp(tot)


def ssd_prep(dt_raw, dt_bias, a_log):
    T = dt_raw.shape[0]
    L = SSD_CHUNK
    nc = T // L
    nh = dt_bias.shape[0]
    pad = V7X_LANES - nh
    bias = jnp.pad(dt_bias, (0, pad)).reshape(1, V7X_LANES)
    alog = jnp.pad(a_log, (0, pad)).reshape(1, V7X_LANES)
    sq = jax.ShapeDtypeStruct((nc, V7X_LANES, L), F32)
    sqspec = pl.BlockSpec((1, V7X_LANES, L), lambda i: (i, 0, 0))
    return pl.pallas_call(
        _ssd_prep_kernel,
        grid=(nc,),
        in_specs=[pl.BlockSpec((L, V7X_LANES), lambda i: (i, 0)),
                  pl.BlockSpec((1, V7X_LANES), lambda i: (0, 0)),
                  pl.BlockSpec((1, V7X_LANES), lambda i: (0, 0))],
        out_specs=[pl.BlockSpec((L, SSD_GROUPS * V7X_LANES), lambda i: (i, 0)),
                   sqspec, sqspec, sqspec],
        out_shape=[jax.ShapeDtypeStruct((T, SSD_GROUPS * V7X_LANES), F32), sq, sq, sq],
        compiler_params=_cparams(("parallel",)),
        name="ssd_prep",
    )(dt_raw, bias, alog)


def _ssd_kernel(z_ref, x_ref, b_ref, c_ref, cumg_ref, cumt_ref, wt_ref, etb_ref,
                d_ref, nw_ref, o_ref, st, *, tt, ng):
    L = SSD_CHUNK
    P2 = 2 * SSD_HEADDIM

    @pl.when(pl.program_id(2) == 0)
    def _():
        st[...] = jnp.zeros_like(st)

    r = lax.broadcasted_iota(jnp.int32, (L, L), 0)
    c = lax.broadcasted_iota(jnp.int32, (L, L), 1)
    causal = r >= c
    lo = lax.broadcasted_iota(jnp.int32, (L, P2), 1) < SSD_HEADDIM
    lo_row = lo[0:1, :]

    gw = SSD_HPG * SSD_HEADDIM
    N = SSD_STATE
    groups = range(ng)

    def chunk(ci, carry):
        sl = pl.ds(pl.multiple_of(ci * L, L), L)
        cum_all, w_all, et_all = cumt_ref[ci], wt_ref[ci], etb_ref[ci]
        xs, bts, cbms, olds, yoffs, cumgs, ys = [], [], [], [], [], [], [[] for _ in groups]
        for gi in groups:
            b16 = b_ref[sl, gi * N:(gi + 1) * N]
            c16 = c_ref[sl, gi * N:(gi + 1) * N]
            xs.append(x_ref[sl, gi * gw:(gi + 1) * gw].astype(F32))
            bts.append(b16.astype(F32).T)
            cbms.append(lax.dot_general(c16, b16, (((1,), (1,)), ((), ())),
                                        preferred_element_type=F32))
            olds.append(st[gi])
            yoffs.append(jnp.dot(c16, olds[gi].astype(BF16), preferred_element_type=F32))
            cumgs.append(cumg_ref[sl, gi * V7X_LANES:(gi + 1) * V7X_LANES])
        for p in range(SSD_HPG // 2):
            for gi in groups:
                ms, ws, cols = [], [], []
                for j in (2 * p, 2 * p + 1):
                    row = gi * SSD_HPG + j
                    col = jnp.broadcast_to(cumgs[gi][:, j:j + 1], (L, L))
                    seg = jnp.where(causal, col - cum_all[row:row + 1, :], NEG_BIG)
                    ms.append(cbms[gi] * jnp.exp2(seg))
                    ws.append(bts[gi] * w_all[row:row + 1, :])
                    cols.append(col)
                lhs = jnp.concatenate([jnp.concatenate(ms, axis=1), jnp.concatenate(ws, axis=1)],
                                      axis=0).astype(BF16)
                xp = xs[gi][:, p * P2:(p + 1) * P2]
                x2 = jnp.concatenate([jnp.where(lo, xp, 0.0), jnp.where(lo, 0.0, xp)],
                                     axis=0).astype(BF16)
                res = jnp.dot(lhs, x2, preferred_element_type=F32)
                ecum = jnp.exp2(jnp.where(lo, cols[0], cols[1]))
                ys[gi].append(res[:L] + yoffs[gi][:, p * P2:(p + 1) * P2] * ecum)
                r0 = gi * SSD_HPG + 2 * p
                et = jnp.where(lo_row, et_all[r0:r0 + 1, :], et_all[r0 + 1:r0 + 2, :])
                st[gi, :, p * P2:(p + 1) * P2] = et * olds[gi][:, p * P2:(p + 1) * P2] + res[L:]
        for gi in groups:
            gs = slice(gi * gw, (gi + 1) * gw)
            y = jnp.concatenate(ys[gi], axis=1) + xs[gi] * d_ref[:, gs]
            y = y * z_ref[sl, gs].astype(F32)
            y = y * lax.rsqrt(jnp.mean(y * y, axis=-1, keepdims=True) + EPS) * nw_ref[:, gs]
            o_ref[sl, gs] = y.astype(o_ref.dtype)
        return carry

    lax.fori_loop(0, tt // L, chunk, 0)


def ssd_mixer(act, prep, d_skip, norm_w, *, batch, seq, tt=512):
    T = act.shape[0]
    cumg, cumt, wt, etb = prep
    G = SSD_GROUPS
    gw = SSD_HPG * SSD_HEADDIM
    inner = G * gw
    N = SSD_STATE
    L = SSD_CHUNK
    tt = min(tt, seq)
    nt = seq // tt
    ncb = tt // L
    ng = SSD_GROUPS_PER_STEP
    row = lambda b, g, t: b * nt + t
    bw, sw = ng * gw, ng * N
    xo, bo, co = inner // bw, 2 * inner // sw, (2 * inner + G * N) // sw
    d_exp = jnp.repeat(d_skip, SSD_HEADDIM).reshape(1, inner)
    sq = lambda: pl.BlockSpec((ncb, ng * SSD_HPG, L), lambda b, g, t: (row(b, g, t), g, 0))
    return pl.pallas_call(
        functools.partial(_ssd_kernel, tt=tt, ng=ng),
        grid=(batch, G // ng, nt),
        in_specs=[pl.BlockSpec((tt, bw), lambda b, g, t: (row(b, g, t), g)),
                  pl.BlockSpec((tt, bw), lambda b, g, t: (row(b, g, t), xo + g)),
                  pl.BlockSpec((tt, sw), lambda b, g, t: (row(b, g, t), bo + g)),
                  pl.BlockSpec((tt, sw), lambda b, g, t: (row(b, g, t), co + g)),
                  pl.BlockSpec((tt, ng * V7X_LANES), lambda b, g, t: (row(b, g, t), g)),
                  sq(), sq(), sq(),
                  pl.BlockSpec((1, bw), lambda b, g, t: (0, g)),
                  pl.BlockSpec((1, bw), lambda b, g, t: (0, g))],
        out_specs=pl.BlockSpec((tt, bw), lambda b, g, t: (row(b, g, t), g)),
        out_shape=jax.ShapeDtypeStruct((T, inner), BF16),
        scratch_shapes=[pltpu.VMEM((ng, N, gw), F32)],
        compiler_params=_cparams(("parallel", "parallel", "arbitrary")),
        name="ssd_mixer",
    )(act, act, act, act, cumg, cumt, wt, etb, d_exp, norm_w.reshape(1, inner))


def _rmsnorm_kernel(x_ref, w_ref, o_ref):
    x = x_ref[...]
    ms = jnp.mean(x * x, axis=-1, keepdims=True)
    o_ref[...] = x * lax.rsqrt(ms + EPS) * w_ref[...]


def rmsnorm(x, w, *, tm=256):
    T, D = x.shape
    return pl.pallas_call(
        _rmsnorm_kernel,
        grid=(T // tm,),
        in_specs=[pl.BlockSpec((tm, D), lambda i: (i, 0)), pl.BlockSpec((1, D), lambda i: (0, 0))],
        out_specs=pl.BlockSpec((tm, D), lambda i: (i, 0)),
        out_shape=jax.ShapeDtypeStruct((T, D), F32),
        compiler_params=_cparams(("parallel",)),
        name="final_rmsnorm",
    )(x, w.reshape(1, D))


def even_layer(x, nw, w_in, w_out, e, lru_cw, lru_cb, w_r, b_r, w_i, b_i, lam, lower_bounds, hg_nw,
               *, batch, seq):
    lru_w = lru_cw.shape[1]
    proj = norm_matmul(x, nw, w_in, e)
    ya = lru_mixer(proj, lru_cw, lru_cb, w_r, b_r, w_i, b_i, lam, batch=batch, seq=seq)
    yb = hgrn2_mixer(proj, lower_bounds, hg_nw, layer=e, batch=batch, seq=seq, col0=2 * lru_w)
    return matmul_residual([ya, yb], w_out, e, x, tn=w_out.shape[2])


def odd_layer(x, nw, w_in, w_out, o, w_dt, conv_w, conv_b, dt_bias, a_log, d_skip, norm_w,
              *, batch, seq):
    kw, conv_dim = conv_w.shape
    inner = norm_w.shape[0]
    nh = dt_bias.shape[0]
    ident = jnp.zeros((kw, inner), F32).at[kw - 1].set(1.0)
    cw = jnp.concatenate([ident, conv_w], axis=1)[None]
    cb = jnp.concatenate([jnp.zeros((inner,), F32), conv_b]).reshape(1, 1, inner + conv_dim)
    aux_w = jnp.pad(w_dt, ((0, 0), (0, V7X_LANES - nh))).astype(BF16)
    act, dt_raw = proj_conv(x, nw, w_in, o, cw, cb, 0, nb=1, seq=seq, tc=IN_PROJ_TN, aux_w=aux_w)
    prep = ssd_prep(dt_raw, dt_bias, a_log)
    y = ssd_mixer(act, prep, d_skip, norm_w, batch=batch, seq=seq)
    return matmul_residual([y], w_out, o, x)


def ffn_layer(x, nw, w_up, w_down, l, conv_w, conv_b, *, seq):
    act = ffn_up(x, nw, w_up, l, conv_w, conv_b, seq=seq)
    return matmul_residual([act], w_down, l, x)


def kernel(x, norm_mix_w, norm_ffn_w, norm_final_w, ev_w_in, lru_conv_w, lru_conv_b, lru_w_r, lru_b_r,
           lru_w_i, lru_b_i, lru_lambda, hg_lower_bounds, hg_norm_w, ev_w_out, ssd_w_in, ssd_conv_w,
           ssd_conv_b, ssd_dt_bias, ssd_a_log, ssd_d, ssd_norm_w, ssd_w_out, ffn_w_up, ffn_conv_w,
           ffn_conv_b, ffn_w_down):
    batch, seq, d = x.shape
    depth = norm_mix_w.shape[0]
    nh = ssd_dt_bias.shape[1]
    ev_in, ev_out = cast_bf16(ev_w_in), cast_bf16(ev_w_out)
    ssd_in = ssd_w_in[:, :, :ssd_w_in.shape[2] - nh].astype(BF16)
    ssd_out = cast_bf16(ssd_w_out)
    up, down = cast_bf16(ffn_w_up), cast_bf16(ffn_w_down)
    ffn_cb = ffn_conv_b.reshape(depth, 1, -1)
    h = x.reshape(batch * seq, d)
    for l in range(depth):
        if l % 2 == 0:
            e = l // 2
            h = even_layer(h, norm_mix_w[l], ev_in, ev_out, e, lru_conv_w[e], lru_conv_b[e], lru_w_r[e],
                           lru_b_r[e], lru_w_i[e], lru_b_i[e], lru_lambda[e], hg_lower_bounds,
                           hg_norm_w[e], batch=batch, seq=seq)
        else:
            o = l // 2
            h = odd_layer(h, norm_mix_w[l], ssd_in, ssd_out, o, ssd_w_in[o][:, -nh:], ssd_conv_w[o],
                          ssd_conv_b[o], ssd_dt_bias[o], ssd_a_log[o], ssd_d[o], ssd_norm_w[o],
                          batch=batch, seq=seq)
        h = ffn_layer(h, norm_ffn_w[l], up, down, l, ffn_conv_w, ffn_cb, seq=seq)
    return rmsnorm(h, norm_final_w).reshape(batch, seq, d)
```

```python
import functools

import jax
import jax.numpy as jnp
from jax import lax
from jax.experimental import pallas as pl
from jax.experimental.pallas import tpu as pltpu

F32 = jnp.float32
BF16 = jnp.bfloat16
EPS = 1e-6

V7X_LANES = 128
V7X_SUBLANES = 8
V7X_MXU_DIM = 256
V7X_VMEM_BYTES = 64 * 1024 * 1024
VMEM_LIMIT = V7X_VMEM_BYTES - 8 * 1024 * 1024

LRU_BLOCKS = 16
LRU_C = 8.0
HG_HEADS = 8
HG_DK = 128
HG_CHUNK = 64
SSD_HEADDIM = 64
SSD_GROUPS = 8
SSD_HPG = 8
SSD_STATE = 128
SSD_CHUNK = 128
SSD_GROUPS_PER_STEP = 2
NEG_BIG = -1e30
LOG2E = 1.4426950408889634

ROW_TILE = 1024
NORM_ROWS = 128
IN_PROJ_TN = 1024
FFN_TC = 512
CONV_SUB = V7X_MXU_DIM
CONV_BLOCK = 1024 * 256
OUT_TM = 512
OUT_TN = 1024


def _cparams(sem):
    return pltpu.CompilerParams(dimension_semantics=sem, vmem_limit_bytes=VMEM_LIMIT)


def _sigmoid(x):
    return 1.0 / (1.0 + jnp.exp(-x))


def _silu(x):
    return x * _sigmoid(x)


def _gelu_tanh(x):
    return 0.5 * x * (1.0 + jnp.tanh(0.7978845608028654 * (x + 0.044715 * (x * x * x))))


def _split3(x):
    hi = x.astype(BF16)
    r1 = x - hi.astype(F32)
    mid = r1.astype(BF16)
    lo = (r1 - mid.astype(F32)).astype(BF16)
    return jnp.concatenate([hi, mid, lo], axis=1)


def _cumsum_rows(x):
    n, w = x.shape
    r = lax.broadcasted_iota(jnp.int32, (n, n), 0)
    c = lax.broadcasted_iota(jnp.int32, (n, n), 1)
    tril = jnp.where(r >= c, 1.0, 0.0).astype(BF16)
    p = jnp.dot(tril, _split3(x), preferred_element_type=F32)
    return p[:, :w] + p[:, w:2 * w] + p[:, 2 * w:]


def _normalise_rows(x_ref, nw_ref, h_ref, rows):
    def body(r, c):
        sl = pl.ds(pl.multiple_of(r * rows, rows), rows)
        x = x_ref[sl, :]
        ms = jnp.mean(x * x, axis=-1, keepdims=True)
        h_ref[sl, :] = (x * lax.rsqrt(ms + EPS) * nw_ref[...]).astype(BF16)
        return c
    lax.fori_loop(0, x_ref.shape[0] // rows, body, 0)


def _cast_kernel(x_ref, o_ref):
    o_ref[...] = x_ref[...].astype(o_ref.dtype)


def cast_bf16(w, ncols=None):
    L, K, _ = w.shape
    N = w.shape[2] if ncols is None else ncols
    tk = 512
    tn = max(t for t in range(V7X_LANES, 2816 + 1, V7X_LANES) if N % t == 0)
    return pl.pallas_call(
        _cast_kernel,
        grid=(L, K // tk, N // tn),
        in_specs=[pl.BlockSpec((1, tk, tn), lambda l, i, j: (l, i, j))],
        out_specs=pl.BlockSpec((1, tk, tn), lambda l, i, j: (l, i, j)),
        out_shape=jax.ShapeDtypeStruct((L, K, N), BF16),
        compiler_params=_cparams(("parallel", "parallel", "parallel")),
        name="cast_bf16",
    )(w)


def _norm_mm_kernel(x_ref, nw_ref, w_ref, o_ref, h_ref, *, rows):
    @pl.when(pl.program_id(1) == 0)
    def _():
        _normalise_rows(x_ref, nw_ref, h_ref, rows)

    o_ref[...] = jnp.dot(h_ref[...], w_ref[...], preferred_element_type=F32).astype(o_ref.dtype)


def norm_matmul(x, nw, w, layer, *, tm=ROW_TILE, tn=IN_PROJ_TN):
    T, D = x.shape
    N = w.shape[2]
    tm = min(tm, T)
    return pl.pallas_call(
        functools.partial(_norm_mm_kernel, rows=NORM_ROWS),
        grid=(T // tm, N // tn),
        in_specs=[pl.BlockSpec((tm, D), lambda i, j: (i, 0)),
                  pl.BlockSpec((1, D), lambda i, j: (0, 0)),
                  pl.BlockSpec((None, D, tn), lambda i, j: (layer, 0, j))],
        out_specs=pl.BlockSpec((tm, tn), lambda i, j: (i, j)),
        out_shape=jax.ShapeDtypeStruct((T, N), F32),
        scratch_shapes=[pltpu.VMEM((tm, D), BF16)],
        compiler_params=_cparams(("parallel", "arbitrary")),
        name="norm_matmul",
    )(x, nw.reshape(1, D), w)


def _mm_res_kernel(*refs, nl):
    lhs = refs[:nl]
    w_ref, r_ref, o_ref = refs[nl], refs[nl + 1], refs[nl + 2]
    acc = r_ref[...]
    k0 = 0
    for l in lhs:
        k = l.shape[1]
        acc = acc + jnp.dot(l[...], w_ref[k0:k0 + k, :], preferred_element_type=F32)
        k0 += k
    o_ref[...] = acc


def matmul_residual(lhs_list, w, layer, res, *, tm=OUT_TM, tn=OUT_TN):
    T, N = res.shape
    K = w.shape[1]
    tm = min(tm, T)
    nl = len(lhs_list)
    in_specs = [pl.BlockSpec((tm, l.shape[1]), lambda j, i: (i, 0)) for l in lhs_list]
    in_specs += [pl.BlockSpec((None, K, tn), lambda j, i: (layer, 0, j)),
                 pl.BlockSpec((tm, tn), lambda j, i: (i, j))]
    return pl.pallas_call(
        functools.partial(_mm_res_kernel, nl=nl),
        grid=(N // tn, T // tm),
        in_specs=in_specs,
        out_specs=pl.BlockSpec((tm, tn), lambda j, i: (i, j)),
        out_shape=jax.ShapeDtypeStruct((T, N), F32),
        compiler_params=_cparams(("parallel", "parallel")),
        name="matmul_residual",
    )(*lhs_list, w, res)


def _proj_conv_kernel(*refs, nb, nsub, tm, seq, kw, rows, erows, aux, plain):
    it = iter(refs)
    x_ref, nw_ref = next(it), next(it)
    w_refs = [next(it) for _ in range(nb)]
    cw_refs = [next(it) for _ in range(nb)]
    cb_refs = [next(it) for _ in range(nb)]
    aux_w = next(it) if aux else None
    o_ref = next(it)
    aux_o = next(it) if aux else None
    h_ref = next(it)
    pads = [next(it) for _ in range(nb)]
    pad = [[pads[b].at[s] for s in range(nsub)] for b in range(nb)]
    halo = next(it)

    i = pl.program_id(0)
    j = pl.program_id(1)
    sub = o_ref.shape[1] // nsub

    @pl.when(j == 0)
    def _():
        _normalise_rows(x_ref, nw_ref, h_ref, rows)
        if aux:
            aux_o[...] = jnp.dot(h_ref[...], aux_w[...], preferred_element_type=F32)

    first = (i * tm) % seq == 0

    def product(s, b):
        cs = slice(s * sub, (s + 1) * sub)
        p = pad[b][s]
        p[0:8, :] = jnp.where(first, 0.0, halo[j, b, :, cs])
        p[8:8 + tm, :] = jnp.dot(h_ref[...], w_refs[b][:, cs], preferred_element_type=F32)
        halo[j, b, :, cs] = p[tm:tm + 8, :]

    def conv(s, b, r0):
        cs = slice(s * sub, (s + 1) * sub)
        blk = pad[b][s][r0:r0 + erows + 8, :]
        y = cb_refs[b][:, cs] + cw_refs[b][kw - 1:kw, cs] * blk[8:, :]
        for k in range(kw - 1):
            y = y + cw_refs[b][k:k + 1, cs] * pltpu.roll(blk, kw - 1 - k, axis=0)[8:, :]
        return y

    def tile(with_conv):
        for s in range(nsub):
            cs = slice(s * sub, (s + 1) * sub)
            product(s, 0)
            if nb == 2:
                yg = conv(s, 0, 0)
                product(s, 1)
                o_ref[:, cs] = (_silu(yg) * conv(s, 1, 0)).astype(o_ref.dtype)
            else:
                for r0 in range(0, tm, erows):
                    y = conv(s, 0, r0) if with_conv else pad[0][s][r0 + 8:r0 + 8 + erows, :]
                    o_ref[r0:r0 + erows, cs] = _silu(y).astype(o_ref.dtype)

    if plain:
        @pl.when(j < plain)
        def _():
            tile(False)

        @pl.when(j >= plain)
        def _():
            tile(True)
    else:
        tile(True)


def proj_conv(x, nw, w, layer, cw, cb, conv_layer, *, nb, seq, tc, aux_w=None,
              tm=ROW_TILE, sub=CONV_SUB, block=CONV_BLOCK, plain=0):
    T, D = x.shape
    kw = cw.shape[1]
    nj = cw.shape[2] // (nb * tc)
    tm = min(tm, seq)
    nsub = tc // sub
    in_specs = [pl.BlockSpec((tm, D), lambda i, j: (i, 0)), pl.BlockSpec((1, D), lambda i, j: (0, 0))]
    in_specs += [pl.BlockSpec((None, D, tc), functools.partial(lambda i, j, b: (layer, 0, j + b * nj), b=b))
                 for b in range(nb)]
    in_specs += [pl.BlockSpec((None, kw, tc),
                              functools.partial(lambda i, j, b: (conv_layer, 0, j + b * nj), b=b))
                 for b in range(nb)]
    in_specs += [pl.BlockSpec((None, 1, tc),
                              functools.partial(lambda i, j, b: (conv_layer, 0, j + b * nj), b=b))
                 for b in range(nb)]
    args = [x, nw.reshape(1, D)] + [w] * nb + [cw] * nb + [cb] * nb
    out_specs = [pl.BlockSpec((tm, tc), lambda i, j: (i, j))]
    out_shape = [jax.ShapeDtypeStruct((T, nj * tc), BF16)]
    aux = aux_w is not None
    if aux:
        in_specs.append(pl.BlockSpec(aux_w.shape, lambda i, j: (0, 0)))
        args.append(aux_w)
        out_specs.append(pl.BlockSpec((tm, aux_w.shape[1]), lambda i, j: (i, 0)))
        out_shape.append(jax.ShapeDtypeStruct((T, aux_w.shape[1]), F32))
    outs = pl.pallas_call(
        functools.partial(_proj_conv_kernel, nb=nb, nsub=nsub, tm=tm, seq=seq, kw=kw,
                          rows=NORM_ROWS, erows=min(block // sub, tm), aux=aux, plain=plain),
        grid=(T // tm, nj),
        in_specs=in_specs,
        out_specs=out_specs,
        out_shape=out_shape,
        scratch_shapes=([pltpu.VMEM((tm, D), BF16)]
                        + [pltpu.VMEM((nsub, tm + 8, sub), F32) for _ in range(nb)]
                        + [pltpu.VMEM((nj, nb, 8, tc), F32)]),
        compiler_params=_cparams(("arbitrary", "arbitrary")),
        name="proj_conv",
    )(*args)
    return outs if aux else outs[0]


def _ffn_up_kernel(x_ref, nw_ref, wg_ref, wv_ref, cwg_ref, cwv_ref, cbg_ref, cbv_ref, o_ref,
                   h_ref, gpad, vpad, halo, *, tm, seq, kw, rows, sub):
    i = pl.program_id(0)
    j = pl.program_id(1)

    @pl.when(j == 0)
    def _():
        _normalise_rows(x_ref, nw_ref, h_ref, rows)

    first = (i * tm) % seq == 0
    tc = o_ref.shape[1]

    def branch(w_ref, pad, cw_ref, cb_ref, slot, s):
        cs = slice(s * sub, (s + 1) * sub)
        pad[s, 0:8, :] = jnp.where(first, 0.0, halo[j, slot, :, cs])
        pad[s, 8:8 + tm, :] = jnp.dot(h_ref[...], w_ref[:, cs], preferred_element_type=F32)
        halo[j, slot, :, cs] = pad[s, tm:tm + 8, :]
        y = cb_ref[:, cs] + cw_ref[kw - 1:kw, cs] * pad[s, 8:8 + tm, :]
        for k in range(kw - 1):
            y = y + cw_ref[k:k + 1, cs] * pad[s, pl.ds(8 - (kw - 1) + k, tm), :]
        return y

    for s in range(tc // sub):
        g = branch(wg_ref, gpad, cwg_ref, cbg_ref, 0, s)
        v = branch(wv_ref, vpad, cwv_ref, cbv_ref, 1, s)
        o_ref[:, s * sub:(s + 1) * sub] = (_silu(g) * v).astype(o_ref.dtype)


def ffn_up(x, nw, w_up, layer, cw, cb, *, seq, tm=ROW_TILE, tc=FFN_TC, sub=CONV_SUB):
    T, D = x.shape
    F = w_up.shape[2] // 2
    kw = cw.shape[1]
    tm = min(tm, seq)
    nj = F // tc
    wspec = lambda off: pl.BlockSpec((None, D, tc), lambda i, j: (layer, 0, j + off))
    cwspec = lambda off: pl.BlockSpec((None, kw, tc), lambda i, j: (layer, 0, j + off))
    cbspec = lambda off: pl.BlockSpec((None, 1, tc), lambda i, j: (layer, 0, j + off))
    return pl.pallas_call(
        functools.partial(_ffn_up_kernel, tm=tm, seq=seq, kw=kw, rows=NORM_ROWS, sub=sub),
        grid=(T // tm, nj),
        in_specs=[pl.BlockSpec((tm, D), lambda i, j: (i, 0)),
                  pl.BlockSpec((1, D), lambda i, j: (0, 0)),
                  wspec(0), wspec(nj), cwspec(0), cwspec(nj), cbspec(0), cbspec(nj)],
        out_specs=pl.BlockSpec((tm, tc), lambda i, j: (i, j)),
        out_shape=jax.ShapeDtypeStruct((T, F), BF16),
        scratch_shapes=[pltpu.VMEM((tm, D), BF16),
                        pltpu.VMEM((tc // sub, tm + 8, sub), F32),
                        pltpu.VMEM((tc // sub, tm + 8, sub), F32),
                        pltpu.VMEM((nj, 2, 8, tc), F32)],
        compiler_params=_cparams(("arbitrary", "arbitrary")),
        name="ffn_up",
    )(x, nw.reshape(1, D), w_up, w_up, cw, cw, cb, cb)


def _lru_kernel(xa_ref, ga_ref, cw_ref, cb_ref, wr_ref, br_ref, wi_ref, bi_ref, lam_ref, o_ref,
                xpad, a_s, u_s, hc, *, tt, kw):
    W = xa_ref.shape[1]

    @pl.when(pl.program_id(1) == 0)
    def _():
        xpad[0:8, :] = jnp.zeros((8, W), F32)
        hc[...] = jnp.zeros_like(hc)

    xpad[8:8 + tt, :] = xa_ref[...]
    full = xpad[...]
    xc = cb_ref[...] + cw_ref[kw - 1:kw, :] * full[8:, :]
    for k in range(kw - 1):
        xc = xc + cw_ref[k:k + 1, :] * pltpu.roll(full, kw - 1 - k, axis=0)[8:, :]
    xpad[0:8, :] = full[tt:tt + 8, :]

    xb = xc.astype(BF16)
    nt = W // V7X_MXU_DIM

    def gate(w_ref, b_ref):
        parts = [jnp.dot(xb[:, j * V7X_MXU_DIM:(j + 1) * V7X_MXU_DIM], w_ref[j],
                         preferred_element_type=F32) for j in range(nt)]
        return _sigmoid(jnp.concatenate(parts, axis=1) + b_ref[...])

    r = gate(wr_ref, br_ref)
    gi = gate(wi_ref, bi_ref)
    nl = -lam_ref[...]
    sp = jnp.maximum(nl, 0.0) + jnp.log1p(jnp.exp(-jnp.abs(nl)))
    log_a = (-LRU_C) * r * sp
    a_s[...] = jnp.exp(log_a)
    th = jnp.tanh(log_a)
    u_s[...] = jnp.sqrt(-2.0 * th / (1.0 - th)) * (gi * xc)

    rows = lax.broadcasted_iota(jnp.int32, (8, W), 0)

    def body(i, h):
        sl = pl.ds(pl.multiple_of(i * 8, 8), 8)
        a = a_s[sl, :]
        u = u_s[sl, :]
        for d in (1, 2, 4):
            m = rows >= d
            a_sh = pltpu.roll(a, d, axis=0)
            u_sh = pltpu.roll(u, d, axis=0)
            u = jnp.where(m, a * u_sh + u, u)
            a = jnp.where(m, a * a_sh, a)
        hh = a * h + u
        a_s[sl, :] = hh
        return hh[7:8, :]

    hc[0:1, :] = lax.fori_loop(0, tt // 8, body, hc[0:1, :])
    o_ref[...] = (a_s[...] * _gelu_tanh(ga_ref[...])).astype(o_ref.dtype)


def _blockdiag_tiles(w):
    nb, k, _ = w.shape
    per = V7X_MXU_DIM // k
    w4 = w.reshape(nb // per, per, k, k)
    eye = jnp.eye(per, dtype=w.dtype)
    t = jnp.einsum('tbij,bc->tbicj', w4, eye)
    return t.reshape(nb // per, V7X_MXU_DIM, V7X_MXU_DIM)


def lru_mixer(proj, cw, cb, w_r, b_r, w_i, b_i, lam, *, batch, seq, tt=256):
    T = proj.shape[0]
    kw, W = cw.shape
    tt = min(tt, seq)
    nt = seq // tt
    wr = _blockdiag_tiles(w_r).astype(BF16)
    wi = _blockdiag_tiles(w_i).astype(BF16)
    vec = lambda b, t: (0, 0)
    wspec = pl.BlockSpec(wr.shape, lambda b, t: (0, 0, 0))
    return pl.pallas_call(
        functools.partial(_lru_kernel, tt=tt, kw=kw),
        grid=(batch, nt),
        in_specs=[pl.BlockSpec((tt, W), lambda b, t: (b * nt + t, 0)),
                  pl.BlockSpec((tt, W), lambda b, t: (b * nt + t, 1)),
                  pl.BlockSpec((kw, W), vec), pl.BlockSpec((1, W), vec),
                  wspec, pl.BlockSpec((1, W), vec),
                  wspec, pl.BlockSpec((1, W), vec),
                  pl.BlockSpec((1, W), vec)],
        out_specs=pl.BlockSpec((tt, W), lambda b, t: (b * nt + t, 0)),
        out_shape=jax.ShapeDtypeStruct((T, W), BF16),
        scratch_shapes=[pltpu.VMEM((tt + 8, W), F32), pltpu.VMEM((tt, W), F32),
                        pltpu.VMEM((tt, W), F32), pltpu.VMEM((8, W), F32)],
        compiler_params=_cparams(("parallel", "arbitrary")),
        name="lru_mixer",
    )(proj, proj, cw, cb.reshape(1, W), wr, b_r.reshape(1, W), wi, b_i.reshape(1, W),
      lam.reshape(1, W))


def _hgrn2_kernel(q_ref, f_ref, v_ref, g_ref, lb_ref, nw_ref, o_ref, st, *, tt, layer, hb):
    C = HG_CHUNK

    @pl.when(pl.program_id(2) == 0)
    def _():
        st[...] = jnp.zeros_like(st)

    raw = lb_ref[...]
    e = jnp.exp(raw - jnp.max(raw, axis=0, keepdims=True))
    p = e / jnp.sum(e, axis=0, keepdims=True)
    lb_all = -p[0:1, :]
    for i in range(layer + 1):
        lb_all = lb_all + p[i:i + 1, :]

    r = lax.broadcasted_iota(jnp.int32, (C, C), 0)
    c = lax.broadcasted_iota(jnp.int32, (C, C), 1)
    causal = r >= c
    scale = HG_DK ** -0.5
    nw = nw_ref[...]

    nt_dims = (((1,), (1,)), ((), ()))
    tn_dims = (((0,), (0,)), ((), ()))
    heads = range(hb)
    cols = [slice(h * HG_DK, (h + 1) * HG_DK) for h in heads]

    def chunk(ci, carry):
        sl = pl.ds(pl.multiple_of(ci * C, C), C)
        ks, cums = [], []
        for h in heads:
            lb = lb_all[:, cols[h]]
            f = lb + (1.0 - lb) * _sigmoid(f_ref[sl, cols[h]])
            ks.append(1.0 - f)
            cums.append(_cumsum_rows(jnp.log(f)))
        qss, scs = [], []
        for h in heads:
            cum, k = cums[h], ks[h]
            mid = cum[C // 2 - 1:C // 2, :]
            qs = _silu(q_ref[sl, cols[h]]) * scale
            qd = (qs * jnp.exp(cum - mid)).astype(BF16)
            kd = (k * jnp.exp(mid - cum)).astype(BF16)
            qss.append(qs)
            scs.append(lax.dot_general(qd, kd, nt_dims, preferred_element_type=F32))
        outs = []
        for h in heads:
            cum, k = cums[h], ks[h]
            tot = cum[C - 1:C, :]
            vb = v_ref[sl, cols[h]].astype(BF16)
            sc = jnp.where(causal, scs[h], 0.0).astype(BF16)
            s_old = st[h]
            o = jnp.dot(sc, vb, preferred_element_type=F32)
            o = o + lax.dot_general((qss[h] * jnp.exp(cum)).astype(BF16), s_old.astype(BF16),
                                    nt_dims, preferred_element_type=F32)
            kdec = (k * jnp.exp(tot - cum)).astype(BF16)
            upd = lax.dot_general(vb, kdec, tn_dims, preferred_element_type=F32)
            st[h] = jnp.exp(tot) * s_old + upd
            outs.append(o)
        for h in heads:
            o = outs[h]
            o = o * lax.rsqrt(jnp.mean(o * o, axis=-1, keepdims=True) + EPS) * nw
            o_ref[sl, cols[h]] = (o * _silu(g_ref[sl, cols[h]])).astype(o_ref.dtype)
        return carry

    lax.fori_loop(0, tt // C, chunk, 0)


def hgrn2_mixer(proj, lower_bounds, norm_w, *, layer, batch, seq, col0, tt=256, hb=8):
    T = proj.shape[0]
    tt = min(tt, seq)
    nt = seq // tt
    H = HG_HEADS
    bw = hb * HG_DK
    c0 = col0 // bw
    nhb = H // hb

    def col(k):
        return pl.BlockSpec((tt, bw), lambda b, h, t: (b * nt + t, c0 + k * nhb + h))

    ne = lower_bounds.shape[0]
    return pl.pallas_call(
        functools.partial(_hgrn2_kernel, tt=tt, layer=layer, hb=hb),
        grid=(batch, nhb, nt),
        in_specs=[col(0), col(1), col(2), col(3),
                  pl.BlockSpec((ne, bw), lambda b, h, t: (0, h)),
                  pl.BlockSpec((1, HG_DK), lambda b, h, t: (0, 0))],
        out_specs=pl.BlockSpec((tt, bw), lambda b, h, t: (b * nt + t, h)),
        out_shape=jax.ShapeDtypeStruct((T, H * HG_DK), BF16),
        scratch_shapes=[pltpu.VMEM((hb, HG_DK, HG_DK), F32)],
        compiler_params=_cparams(("parallel", "parallel", "arbitrary")),
        name="hgrn2_mixer",
    )(proj, proj, proj, proj, lower_bounds, norm_w.reshape(1, HG_DK))


def _ssd_prep_kernel(dt_ref, bias_ref, alog_ref, cumg_ref, cumt_ref, wt_ref, etb_ref):
    L = SSD_CHUNK
    x = dt_ref[...] + bias_ref[...]
    dt = jnp.maximum(x, 0.0) + jnp.log1p(jnp.exp(-jnp.abs(x)))
    a_neg = -jnp.exp(alog_ref[...])
    cum = _cumsum_rows(dt * a_neg)
    col2 = cum * LOG2E
    for g in range(SSD_GROUPS):
        sh = (V7X_LANES - SSD_HPG * g) % V7X_LANES
        cumg_ref[:, g * V7X_LANES:(g + 1) * V7X_LANES] = col2 if sh == 0 else pltpu.roll(col2, sh, axis=1)
    cum_t = cum.T
    dt_t = dt.T
    tot = jnp.broadcast_to(cum_t[:, L - 1:L], cum_t.shape)
    cumt_ref[0] = (cum_t - jnp.log(dt_t)) * LOG2E
    wt_ref[0] = dt_t * jnp.exp(tot - cum_t)
    etb_ref[0] = jnp.exp(tot)


def ssd_prep(dt_raw, dt_bias, a_log):
    T = dt_raw.shape[0]
    L = SSD_CHUNK
    nc = T // L
    nh = dt_bias.shape[0]
    pad = V7X_LANES - nh
    bias = jnp.pad(dt_bias, (0, pad)).reshape(1, V7X_LANES)
    alog = jnp.pad(a_log, (0, pad)).reshape(1, V7X_LANES)
    sq = jax.ShapeDtypeStruct((nc, V7X_LANES, L), F32)
    sqspec = pl.BlockSpec((1, V7X_LANES, L), lambda i: (i, 0, 0))
    return pl.pallas_call(
        _ssd_prep_kernel,
        grid=(nc,),
        in_specs=[pl.BlockSpec((L, V7X_LANES), lambda i: (i, 0)),
                  pl.BlockSpec((1, V7X_LANES), lambda i: (0, 0)),
                  pl.BlockSpec((1, V7X_LANES), lambda i: (0, 0))],
        out_specs=[pl.BlockSpec((L, SSD_GROUPS * V7X_LANES), lambda i: (i, 0)),
                   sqspec, sqspec, sqspec],
        out_shape=[jax.ShapeDtypeStruct((T, SSD_GROUPS * V7X_LANES), F32), sq, sq, sq],
        compiler_params=_cparams(("parallel",)),
        name="ssd_prep",
    )(dt_raw, bias, alog)


def _ssd_kernel(z_ref, x_ref, b_ref, c_ref, cumg_ref, cumt_ref, wt_ref, etb_ref,
                d_ref, nw_ref, o_ref, st, *, tt, ng):
    L = SSD_CHUNK
    P2 = 2 * SSD_HEADDIM

    @pl.when(pl.program_id(2) == 0)
    def _():
        st[...] = jnp.zeros_like(st)

    r = lax.broadcasted_iota(jnp.int32, (L, L), 0)
    c = lax.broadcasted_iota(jnp.int32, (L, L), 1)
    causal = r >= c
    lo = lax.broadcasted_iota(jnp.int32, (L, P2), 1) < SSD_HEADDIM
    lo_row = lo[0:1, :]

    gw = SSD_HPG * SSD_HEADDIM
    N = SSD_STATE
    groups = range(ng)

    def chunk(ci, carry):
        sl = pl.ds(pl.multiple_of(ci * L, L), L)
        cum_all, w_all, et_all = cumt_ref[ci], wt_ref[ci], etb_ref[ci]
        xs, bts, cbms, olds, yoffs, cumgs, ys = [], [], [], [], [], [], [[] for _ in groups]
        for gi in groups:
            b16 = b_ref[sl, gi * N:(gi + 1) * N]
            c16 = c_ref[sl, gi * N:(gi + 1) * N]
            xs.append(x_ref[sl, gi * gw:(gi + 1) * gw].astype(F32))
            bts.append(b16.astype(F32).T)
            cbms.append(lax.dot_general(c16, b16, (((1,), (1,)), ((), ())),
                                        preferred_element_type=F32))
            olds.append(st[gi])
            yoffs.append(jnp.dot(c16, olds[gi].astype(BF16), preferred_element_type=F32))
            cumgs.append(cumg_ref[sl, gi * V7X_LANES:(gi + 1) * V7X_LANES])
        for p in range(SSD_HPG // 2):
            for gi in groups:
                ms, ws, cols = [], [], []
                for j in (2 * p, 2 * p + 1):
                    row = gi * SSD_HPG + j
                    col = jnp.broadcast_to(cumgs[gi][:, j:j + 1], (L, L))
                    seg = jnp.where(causal, col - cum_all[row:row + 1, :], NEG_BIG)
                    ms.append(cbms[gi] * jnp.exp2(seg))
                    ws.append(bts[gi] * w_all[row:row + 1, :])
                    cols.append(col)
                lhs = jnp.concatenate([jnp.concatenate(ms, axis=1), jnp.concatenate(ws, axis=1)],
                                      axis=0).astype(BF16)
                xp = xs[gi][:, p * P2:(p + 1) * P2]
                x2 = jnp.concatenate([jnp.where(lo, xp, 0.0), jnp.where(lo, 0.0, xp)],
                                     axis=0).astype(BF16)
                res = jnp.dot(lhs, x2, preferred_element_type=F32)
                ecum = jnp.exp2(jnp.where(lo, cols[0], cols[1]))
                ys[gi].append(res[:L] + yoffs[gi][:, p * P2:(p + 1) * P2] * ecum)
                r0 = gi * SSD_HPG + 2 * p
                et = jnp.where(lo_row, et_all[r0:r0 + 1, :], et_all[r0 + 1:r0 + 2, :])
                st[gi, :, p * P2:(p + 1) * P2] = et * olds[gi][:, p * P2:(p + 1) * P2] + res[L:]
        for gi in groups:
            gs = slice(gi * gw, (gi + 1) * gw)
            y = jnp.concatenate(ys[gi], axis=1) + xs[gi] * d_ref[:, gs]
            y = y * z_ref[sl, gs].astype(F32)
            y = y * lax.rsqrt(jnp.mean(y * y, axis=-1, keepdims=True) + EPS) * nw_ref[:, gs]
            o_ref[sl, gs] = y.astype(o_ref.dtype)
        return carry

    lax.fori_loop(0, tt // L, chunk, 0)


def ssd_mixer(act, prep, d_skip, norm_w, *, batch, seq, tt=512):
    T = act.shape[0]
    cumg, cumt, wt, etb = prep
    G = SSD_GROUPS
    gw = SSD_HPG * SSD_HEADDIM
    inner = G * gw
    N = SSD_STATE
    L = SSD_CHUNK
    tt = min(tt, seq)
    nt = seq // tt
    ncb = tt // L
    ng = SSD_GROUPS_PER_STEP
    row = lambda b, g, t: b * nt + t
    bw, sw = ng * gw, ng * N
    xo, bo, co = inner // bw, 2 * inner // sw, (2 * inner + G * N) // sw
    d_exp = jnp.repeat(d_skip, SSD_HEADDIM).reshape(1, inner)
    sq = lambda: pl.BlockSpec((ncb, ng * SSD_HPG, L), lambda b, g, t: (row(b, g, t), g, 0))
    return pl.pallas_call(
        functools.partial(_ssd_kernel, tt=tt, ng=ng),
        grid=(batch, G // ng, nt),
        in_specs=[pl.BlockSpec((tt, bw), lambda b, g, t: (row(b, g, t), g)),
                  pl.BlockSpec((tt, bw), lambda b, g, t: (row(b, g, t), xo + g)),
                  pl.BlockSpec((tt, sw), lambda b, g, t: (row(b, g, t), bo + g)),
                  pl.BlockSpec((tt, sw), lambda b, g, t: (row(b, g, t), co + g)),
                  pl.BlockSpec((tt, ng * V7X_LANES), lambda b, g, t: (row(b, g, t), g)),
                  sq(), sq(), sq(),
                  pl.BlockSpec((1, bw), lambda b, g, t: (0, g)),
                  pl.BlockSpec((1, bw), lambda b, g, t: (0, g))],
        out_specs=pl.BlockSpec((tt, bw), lambda b, g, t: (row(b, g, t), g)),
        out_shape=jax.ShapeDtypeStruct((T, inner), BF16),
        scratch_shapes=[pltpu.VMEM((ng, N, gw), F32)],
        compiler_params=_cparams(("parallel", "parallel", "arbitrary")),
        name="ssd_mixer",
    )(act, act, act, act, cumg, cumt, wt, etb, d_exp, norm_w.reshape(1, inner))


def _rmsnorm_kernel(x_ref, w_ref, o_ref):
    x = x_ref[...]
    ms = jnp.mean(x * x, axis=-1, keepdims=True)
    o_ref[...] = x * lax.rsqrt(ms + EPS) * w_ref[...]


def rmsnorm(x, w, *, tm=256):
    T, D = x.shape
    return pl.pallas_call(
        _rmsnorm_kernel,
        grid=(T // tm,),
        in_specs=[pl.BlockSpec((tm, D), lambda i: (i, 0)), pl.BlockSpec((1, D), lambda i: (0, 0))],
        out_specs=pl.BlockSpec((tm, D), lambda i: (i, 0)),
        out_shape=jax.ShapeDtypeStruct((T, D), F32),
        compiler_params=_cparams(("parallel",)),
        name="final_rmsnorm",
    )(x, w.reshape(1, D))


def even_layer(x, nw, w_in, w_out, e, lru_cw, lru_cb, w_r, b_r, w_i, b_i, lam, lower_bounds, hg_nw,
               *, batch, seq):
    lru_w = lru_cw.shape[1]
    proj = norm_matmul(x, nw, w_in, e)
    ya = lru_mixer(proj, lru_cw, lru_cb, w_r, b_r, w_i, b_i, lam, batch=batch, seq=seq)
    yb = hgrn2_mixer(proj, lower_bounds, hg_nw, layer=e, batch=batch, seq=seq, col0=2 * lru_w)
    return matmul_residual([ya, yb], w_out, e, x, tn=w_out.shape[2])


def odd_layer(x, nw, w_in, w_out, o, w_dt, conv_w, conv_b, dt_bias, a_log, d_skip, norm_w,
              *, batch, seq):
    kw, conv_dim = conv_w.shape
    inner = norm_w.shape[0]
    nh = dt_bias.shape[0]
    ident = jnp.zeros((kw, inner), F32).at[kw - 1].set(1.0)
    cw = jnp.concatenate([ident, conv_w], axis=1)[None]
    cb = jnp.concatenate([jnp.zeros((inner,), F32), conv_b]).reshape(1, 1, inner + conv_dim)
    aux_w = jnp.pad(w_dt, ((0, 0), (0, V7X_LANES - nh))).astype(BF16)
    act, dt_raw = proj_conv(x, nw, w_in, o, cw, cb, 0, nb=1, seq=seq, tc=IN_PROJ_TN, aux_w=aux_w,
                            plain=inner // IN_PROJ_TN)
    prep = ssd_prep(dt_raw, dt_bias, a_log)
    y = ssd_mixer(act, prep, d_skip, norm_w, batch=batch, seq=seq)
    return matmul_residual([y], w_out, o, x)


def ffn_layer(x, nw, w_up, w_down, l, conv_w, conv_b, *, seq):
    act = ffn_up(x, nw, w_up, l, conv_w, conv_b, seq=seq)
    return matmul_residual([act], w_down, l, x)


def kernel(x, norm_mix_w, norm_ffn_w, norm_final_w, ev_w_in, lru_conv_w, lru_conv_b, lru_w_r, lru_b_r,
           lru_w_i, lru_b_i, lru_lambda, hg_lower_bounds, hg_norm_w, ev_w_out, ssd_w_in, ssd_conv_w,
           ssd_conv_b, ssd_dt_bias, ssd_a_log, ssd_d, ssd_norm_w, ssd_w_out, ffn_w_up, ffn_conv_w,
           ffn_conv_b, ffn_w_down):
    batch, seq, d = x.shape
    depth = norm_mix_w.shape[0]
    nh = ssd_dt_bias.shape[1]
    ev_in, ev_out = cast_bf16(ev_w_in), cast_bf16(ev_w_out)
    ssd_in = ssd_w_in[:, :, :ssd_w_in.shape[2] - nh].astype(BF16)
    ssd_out = cast_bf16(ssd_w_out)
    up, down = cast_bf16(ffn_w_up), cast_bf16(ffn_w_down)
    ffn_cb = ffn_conv_b.reshape(depth, 1, -1)
    h = x.reshape(batch * seq, d)
    for l in range(depth):
        if l % 2 == 0:
            e = l // 2
            h = even_layer(h, norm_mix_w[l], ev_in, ev_out, e, lru_conv_w[e], lru_conv_b[e], lru_w_r[e],
                           lru_b_r[e], lru_w_i[e], lru_b_i[e], lru_lambda[e], hg_lower_bounds,
                           hg_norm_w[e], batch=batch, seq=seq)
        else:
            o = l // 2
            h = odd_layer(h, norm_mix_w[l], ssd_in, ssd_out, o, ssd_w_in[o][:, -nh:], ssd_conv_w[o],
                          ssd_conv_b[o], ssd_dt_bias[o], ssd_a_log[o], ssd_d[o], ssd_norm_w[o],
                          batch=batch, seq=seq)
        h = ffn_layer(h, norm_ffn_w[l], up, down, l, ffn_conv_w, ffn_cb, seq=seq)
    return rmsnorm(h, norm_final_w).reshape(batch, seq, d)
```
